```python
import math
import jax
import jax.numpy as jnp
from jax import lax
import numpy as np

D_MODEL = 1024
BATCH = 8
SEQ = 4096
DEPTH = 2

CTX_LEN = 256
GRID_W = 64
D_FF = 4 * D_MODEL
N_BRANCH = 3
NORM_EPS = 1e-6

CONV_DIM = D_MODEL // 2
CONV_WIDTH = 31
LN_EPS = 1e-5

ATT_HEADS = 4
ATT_HEAD_DIM = 64
ATT_VAL_DIM = 2 * ATT_HEAD_DIM
ATT_DIM = ATT_HEADS * 2 * ATT_HEAD_DIM
Q_BLOCK = 128
ROPE_BASE = 10000.0
SUBLN_EPS = 1e-5

RWKV_HEADS = 8
RWKV_HEAD_DIM = 64
RWKV_DIM = RWKV_HEADS * RWKV_HEAD_DIM
DECAY_LORA = 64
ICLR_LORA = 64
GATE_LORA = 128
SHIFT_WIDTH = 3
RWKV_IN = 3 * RWKV_DIM + 2 * DECAY_LORA + 2 * ICLR_LORA + GATE_LORA
GN_EPS = 64e-5

IN_SPLITS = (2 * CONV_DIM, ATT_DIM, ATT_DIM, ATT_DIM, RWKV_IN, N_BRANCH * D_MODEL)
RWKV_SPLITS = (RWKV_DIM, RWKV_DIM, RWKV_DIM, DECAY_LORA, DECAY_LORA, ICLR_LORA, ICLR_LORA, GATE_LORA)
D_IN = 2 * CONV_DIM + 3 * ATT_DIM + RWKV_IN + N_BRANCH * D_MODEL

kernel_name = 'hybrid_conv_diffattn_rwkv7_prefix_dit'


def _split(x, sizes):
    cuts, acc = [], 0
    for s in sizes[:-1]:
        acc += s
        cuts.append(acc)
    return jnp.split(x, cuts, axis=-1)


def _rmsnorm(x, gain, eps=NORM_EPS):
    xf = x.astype(jnp.float32)
    y = xf * lax.rsqrt(jnp.mean(xf * xf, axis=-1, keepdims=True) + eps)
    return (y * gain.astype(jnp.float32)).astype(x.dtype)


def _layernorm(x, gain, bias, eps=LN_EPS):
    xf = x.astype(jnp.float32)
    mu = jnp.mean(xf, axis=-1, keepdims=True)
    var = jnp.mean(jnp.square(xf - mu), axis=-1, keepdims=True)
    y = (xf - mu) * lax.rsqrt(var + eps) * gain.astype(jnp.float32) + bias.astype(jnp.float32)
    return y.astype(x.dtype)


def _modulate(x, gain, shift, scale):
    return _rmsnorm(x, gain) * (1.0 + scale) + shift


def _depthwise_conv(x, w):
    return lax.conv_general_dilated(
        x, w[:, None, :].astype(x.dtype), window_strides=(1,), padding='SAME',
        dimension_numbers=('NWC', 'WIO', 'NWC'), feature_group_count=x.shape[-1])


def _sqrelu_mlp(h, w1, w2):
    return jnp.square(jax.nn.relu(h @ w1)) @ w2


def _conv_branch(z, dw_w, dw_b, ln_g, ln_b):
    a, b = jnp.split(z, 2, axis=-1)
    u = a * jax.nn.sigmoid(b)
    u = _depthwise_conv(u, dw_w) + dw_b
    return jax.nn.silu(_layernorm(u, ln_g, ln_b))


def _axial_rope(seq_len):
    t = jnp.arange(seq_len, dtype=jnp.int32)
    row = (t // GRID_W).astype(jnp.float32)
    col = (t % GRID_W).astype(jnp.float32)
    n_pairs = ATT_HEAD_DIM // 4
    inv_freq = ROPE_BASE ** (-jnp.arange(n_pairs, dtype=jnp.float32) / n_pairs)
    ang = jnp.concatenate([row[:, None] * inv_freq, col[:, None] * inv_freq], axis=-1)
    return jnp.cos(ang), jnp.sin(ang)


def _apply_rope(x, cos, sin):
    xp = x.astype(jnp.float32).reshape(x.shape[:-1] + (ATT_HEAD_DIM // 2, 2))
    x0, x1 = xp[..., 0], xp[..., 1]
    out = jnp.stack([x0 * cos - x1 * sin, x0 * sin + x1 * cos], axis=-1)
    return out.reshape(x.shape).astype(x.dtype)


def _qk_heads(u):
    b, t, _ = u.shape
    return u.reshape(b, t, ATT_HEADS, 2, ATT_HEAD_DIM).transpose(0, 2, 3, 1, 4)


def _v_heads(u):
    b, t, _ = u.shape
    return u.reshape(b, t, ATT_HEADS, ATT_VAL_DIM).transpose(0, 2, 1, 3)


def _diff_lambda(lq1, lk1, lq2, lk2, lambda_init):
    f = jnp.float32
    return (jnp.exp(jnp.sum(lq1.astype(f) * lk1.astype(f)))
            - jnp.exp(jnp.sum(lq2.astype(f) * lk2.astype(f))) + lambda_init)


def _diff_attend(q, k, v, lam):
    b, h, _, tq, dh = q.shape
    nblk = tq // Q_BLOCK
    qb = jnp.moveaxis(q.reshape(b, h, 2, nblk, Q_BLOCK, dh), 3, 0)
    scale = 1.0 / math.sqrt(dh)

    def block(qi):
        s = jnp.einsum('bhcqd,bhckd->bhcqk', qi, k).astype(jnp.float32) * scale
        p = jax.nn.softmax(s, axis=-1)
        a = p[:, :, 0] - lam * p[:, :, 1]
        return jnp.einsum('bhqk,bhkd->bhqd', a.astype(v.dtype), v)

    out = lax.map(block, qb)
    return jnp.moveaxis(out, 0, 2).reshape(b, h, tq, v.shape[-1])


def _att_post(o, subln_g, lambda_init):
    o = _rmsnorm(o, subln_g, SUBLN_EPS) * (1.0 - lambda_init)
    b, h, t, dv = o.shape
    return o.transpose(0, 2, 1, 3).reshape(b, t, h * dv)


def _rwkv_prep(z, lp):
    b, t, _ = z.shape
    r, k, v, wf, wb, af, ab, gd = _split(z, RWKV_SPLITS)

    def heads(u):
        return u.reshape(b, t, RWKV_HEADS, RWKV_HEAD_DIM)

    kkf = heads(k * lp['rwkv_kk']).astype(jnp.float32)
    kk = kkf * lax.rsqrt(jnp.maximum(jnp.sum(kkf * kkf, axis=-1, keepdims=True), 1e-24))
    g = jax.nn.sigmoid(gd) @ lp['rwkv_g2']
    dirs = []
    for d, (wd, ad) in enumerate(((wf, af), (wb, ab))):
        w_log = -jax.nn.softplus(-(lp['rwkv_w0'][d] + jnp.tanh(wd) @ lp['rwkv_w2'][d])) - 0.5
        decay = jnp.exp(-jnp.exp(w_log.astype(jnp.float32)))
        a = jax.nn.sigmoid(lp['rwkv_a0'][d] + ad @ lp['rwkv_a2'][d])
        kd = k * (1.0 + (a - 1.0) * lp['rwkv_ka'])
        dirs.append((heads(decay), heads(kd), heads(a)))
    return heads(r), heads(v), kk, g, dirs


def _rwkv_scan(r, w, k, v, kk, a, s0, reverse):
    xs = tuple(jnp.moveaxis(u.astype(jnp.float32), 1, 0) for u in (r, w, k, v, kk, a))

    def step(s, inp):
        r_t, w_t, k_t, v_t, kk_t, a_t = inp
        sa = jnp.einsum('bhij,bhj->bhi', s, -kk_t)
        s = (s * w_t[:, :, None, :] + sa[..., None] * (kk_t * a_t)[:, :, None, :]
             + v_t[..., None] * k_t[:, :, None, :])
        return s, jnp.einsum('bhij,bhj->bhi', s, r_t)

    s_fin, ys = lax.scan(step, s0, xs, reverse=reverse)
    return jnp.moveaxis(ys, 0, 1), s_fin


def _rwkv_out(ys, r, v, kds, g, lp):
    b, t = ys.shape[:2]
    mu = jnp.mean(ys, axis=-1, keepdims=True)
    var = jnp.mean(jnp.square(ys - mu), axis=-1, keepdims=True)
    yn = ((ys - mu) * lax.rsqrt(var + GN_EPS)).reshape(b, t, RWKV_DIM)
    yn = (yn * lp['rwkv_gn_g'] + lp['rwkv_gn_b']).astype(r.dtype)
    bonus = sum(jnp.sum(r * kd * lp['rwkv_rk'], axis=-1, keepdims=True) * v for kd in kds)
    return (yn + bonus.reshape(b, t, RWKV_DIM)) * g


def _rwkv_branch(zc, zl, lp, need_ctx):
    zc = _depthwise_conv(zc, lp['rwkv_shift'])
    zl = _depthwise_conv(zl, lp['rwkv_shift'])
    rc, vc, kkc, gc, dirs_c = _rwkv_prep(zc, lp)
    rl, vl, kkl, gl, dirs_l = _rwkv_prep(zl, lp)
    s0 = jnp.zeros((zl.shape[0], RWKV_HEADS, RWKV_HEAD_DIM, RWKV_HEAD_DIM), jnp.float32)
    ys_c, ys_l = [], []
    for d, reverse in enumerate((False, True)):
        wc, kc, ac = dirs_c[d]
        wl, kl, al = dirs_l[d]
        yc, s_ctx = _rwkv_scan(rc, wc, kc, vc, kkc, ac, s0, reverse)
        yl, _ = _rwkv_scan(rl, wl, kl, vl, kkl, al, s_ctx, reverse)
        ys_c.append(yc)
        ys_l.append(yl)
    y_l = _rwkv_out(ys_l[0] + ys_l[1], rl, vl, [dirs_l[0][1], dirs_l[1][1]], gl, lp)
    if not need_ctx:
        return None, y_l
    y_c = _rwkv_out(ys_c[0] + ys_c[1], rc, vc, [dirs_c[0][1], dirs_c[1][1]], gc, lp)
    return y_c, y_l


def _merge(y_conv, y_att, y_rwkv, gates, lp):
    g1, g2, g3 = jnp.split(jax.nn.sigmoid(gates), N_BRANCH, axis=-1)
    m = (g1 * (y_conv @ lp['p_conv']) + g2 * (y_att @ lp['p_att'])
         + g3 * (y_rwkv @ lp['p_rwkv']))
    return m @ lp['w_out']


def _token_mixer(hc, hl, lp, lambda_init, cos, sin, need_ctx):
    conv_c, q_c, k_c, v_c, rw_c, gate_c = _split(hc @ lp['w_in'], IN_SPLITS)
    conv_l, q_l, k_l, v_l, rw_l, gate_l = _split(hl @ lp['w_in'], IN_SPLITS)
    conv_args = (lp['conv_dw_w'], lp['conv_dw_b'], lp['conv_ln_g'], lp['conv_ln_b'])
    lam = _diff_lambda(lp['att_lq1'], lp['att_lk1'], lp['att_lq2'], lp['att_lk2'], lambda_init)

    kh_c, vh_c = _qk_heads(k_c), _v_heads(v_c)
    keys_l = jnp.concatenate([kh_c, _apply_rope(_qk_heads(k_l), cos, sin)], axis=3)
    vals_l = jnp.concatenate([vh_c, _v_heads(v_l)], axis=2)
    q_lat = _apply_rope(_qk_heads(q_l), cos, sin)
    y_att_l = _att_post(_diff_attend(q_lat, keys_l, vals_l, lam), lp['att_subln_g'], lambda_init)
    y_conv_l = _conv_branch(conv_l, *conv_args)
    y_rw_c, y_rw_l = _rwkv_branch(rw_c, rw_l, lp, need_ctx)
    out_l = _merge(y_conv_l, y_att_l, y_rw_l, gate_l, lp)
    if not need_ctx:
        return None, out_l
    y_conv_c = _conv_branch(conv_c, *conv_args)
    y_att_c = _att_post(_diff_attend(_qk_heads(q_c), kh_c, vh_c, lam), lp['att_subln_g'], lambda_init)
    out_c = _merge(y_conv_c, y_att_c, y_rw_c, gate_c, lp)
    return out_c, out_l


def setup_inputs(seed: int = 0) -> dict:
    key = jax.random.key(seed)
    keys = jax.random.split(key, 64)
    f32 = jnp.float32
    cnt = [0]

    def nxt():
        cnt[0] += 1
        return keys[cnt[0]]

    def nrm(shape, scale):
        return jax.random.normal(nxt(), shape, f32) * scale

    def gain(shape):
        return 1.0 + nrm(shape, 0.02)

    L, D = DEPTH, D_MODEL
    shift_base = jnp.array([0.25, 0.5, 0.25], f32)[None, :, None]
    return {
        'x': nrm((BATCH, SEQ, D), 1.0),
        'c': nrm((BATCH, D), 1.0),
        'ctx': nrm((BATCH, CTX_LEN, D), 1.0),
        'c_ctx': nrm((D,), 1.0),
        'mod_w': nrm((L, D, 6 * D), D ** -0.5),
        'mod_b': nrm((L, 6 * D), 0.01),
        'norm1_g': gain((L, D)),
        'norm2_g': gain((L, D)),
        'w_in': nrm((L, D, D_IN), D ** -0.5),
        'conv_dw_w': nrm((L, CONV_WIDTH, CONV_DIM), CONV_WIDTH ** -0.5),
        'conv_dw_b': nrm((L, CONV_DIM), 0.01),
        'conv_ln_g': gain((L, CONV_DIM)),
        'conv_ln_b': nrm((L, CONV_DIM), 0.01),
        'p_conv': nrm((L, CONV_DIM, D), CONV_DIM ** -0.5),
        'att_lq1': nrm((L, ATT_HEAD_DIM), 0.1),
        'att_lk1': nrm((L, ATT_HEAD_DIM), 0.1),
        'att_lq2': nrm((L, ATT_HEAD_DIM), 0.1),
        'att_lk2': nrm((L, ATT_HEAD_DIM), 0.1),
        'att_subln_g': gain((L, ATT_VAL_DIM)),
        'p_att': nrm((L, ATT_DIM, D), ATT_DIM ** -0.5),
        'rwkv_shift': shift_base + nrm((L, SHIFT_WIDTH, RWKV_IN), 0.05),
        'rwkv_w0': jax.random.uniform(nxt(), (L, 2, RWKV_DIM), f32, -6.0, -1.0),
        'rwkv_w2': nrm((L, 2, DECAY_LORA, RWKV_DIM), 0.3 * DECAY_LORA ** -0.5),
        'rwkv_a0': nrm((L, 2, RWKV_DIM), 0.1),
        'rwkv_a2': nrm((L, 2, ICLR_LORA, RWKV_DIM), 0.3 * ICLR_LORA ** -0.5),
        'rwkv_g2': nrm((L, GATE_LORA, RWKV_DIM), GATE_LORA ** -0.5),
        'rwkv_kk': 0.85 + nrm((L, RWKV_DIM), 0.05),
        'rwkv_ka': 1.0 + nrm((L, RWKV_DIM), 0.05),
        'rwkv_rk': nrm((L, RWKV_HEADS, RWKV_HEAD_DIM), 0.1),
        'rwkv_gn_g': gain((L, RWKV_DIM)),
        'rwkv_gn_b': nrm((L, RWKV_DIM), 0.01),
        'p_rwkv': nrm((L, RWKV_DIM, D), RWKV_DIM ** -0.5),
        'w_out': nrm((L, D, D), D ** -0.5),
        'mlp_w1': nrm((L, D, D_FF), D ** -0.5),
        'mlp_w2': nrm((L, D_FF, D), D_FF ** -0.5),
        'final_g': gain((D,)),
    }


def reference(x, c, ctx, c_ctx, mod_w, mod_b, norm1_g, norm2_g, w_in,
              conv_dw_w, conv_dw_b, conv_ln_g, conv_ln_b, p_conv,
              att_lq1, att_lk1, att_lq2, att_lk2, att_subln_g, p_att,
              rwkv_shift, rwkv_w0, rwkv_w2, rwkv_a0, rwkv_a2, rwkv_g2,
              rwkv_kk, rwkv_ka, rwkv_rk, rwkv_gn_g, rwkv_gn_b, p_rwkv,
              w_out, mlp_w1, mlp_w2, final_g):
    seq = x.shape[1]
    cos, sin = _axial_rope(seq)
    xl, xc = x, ctx
    silu_c = jax.nn.silu(c)
    silu_cc = jax.nn.silu(c_ctx)
    for l in range(DEPTH):
        need_ctx = l < DEPTH - 1
        lambda_init = 0.8 - 0.6 * math.exp(-0.3 * l)
        lp = dict(w_in=w_in[l], conv_dw_w=conv_dw_w[l], conv_dw_b=conv_dw_b[l],
                  conv_ln_g=conv_ln_g[l], conv_ln_b=conv_ln_b[l], p_conv=p_conv[l],
                  att_lq1=att_lq1[l], att_lk1=att_lk1[l], att_lq2=att_lq2[l], att_lk2=att_lk2[l],
                  att_subln_g=att_subln_g[l], p_att=p_att[l],
                  rwkv_shift=rwkv_shift[l], rwkv_w0=rwkv_w0[l], rwkv_w2=rwkv_w2[l],
                  rwkv_a0=rwkv_a0[l], rwkv_a2=rwkv_a2[l], rwkv_g2=rwkv_g2[l],
                  rwkv_kk=rwkv_kk[l], rwkv_ka=rwkv_ka[l], rwkv_rk=rwkv_rk[l],
                  rwkv_gn_g=rwkv_gn_g[l], rwkv_gn_b=rwkv_gn_b[l], p_rwkv=p_rwkv[l],
                  w_out=w_out[l])
        shl1, scl1, gl1, shl2, scl2, gl2 = jnp.split(
            (silu_c @ mod_w[l] + mod_b[l])[:, None, :], 6, axis=-1)
        shc1, scc1, gc1, shc2, scc2, gc2 = jnp.split(
            (silu_cc @ mod_w[l] + mod_b[l])[None, None, :], 6, axis=-1)
        hl = _modulate(xl, norm1_g[l], shl1, scl1)
        hc = _modulate(xc, norm1_g[l], shc1, scc1)
        oc, ol = _token_mixer(hc, hl, lp, lambda_init, cos, sin, need_ctx)
        xl = xl + gl1 * ol
        xl = xl + gl2 * _sqrelu_mlp(_modulate(xl, norm2_g[l], shl2, scl2), mlp_w1[l], mlp_w2[l])
        if need_ctx:
            xc = xc + gc1 * oc
            xc = xc + gc2 * _sqrelu_mlp(_modulate(xc, norm2_g[l], shc2, scc2), mlp_w1[l], mlp_w2[l])
    return _rmsnorm(xl, final_g)
```

```python
import functools
import math

import jax
import jax.numpy as jnp
from jax import lax
from jax.experimental import pallas as pl
from jax.experimental.pallas import tpu as pltpu

F32 = jnp.float32
BF16 = jnp.bfloat16

GRID_W = 64
NORM_EPS = 1e-6
CONV_WIDTH = 31
LN_EPS = 1e-5
ATT_HEADS = 4
ATT_HEAD_DIM = 64
ATT_VAL_DIM = 2 * ATT_HEAD_DIM
ROPE_BASE = 10000.0
SUBLN_EPS = 1e-5
RWKV_HEADS = 8
RWKV_HEAD_DIM = 64
RWKV_DIM = RWKV_HEADS * RWKV_HEAD_DIM
DECAY_LORA = 64
ICLR_LORA = 64
GATE_LORA = 128
GN_EPS = 64e-5

LANES = 128
HALO = 16
SCAN_CHUNK = 64
VMEM_LIMIT = 56 << 20


def _cparams(*sem):
    return pltpu.CompilerParams(dimension_semantics=sem, vmem_limit_bytes=VMEM_LIMIT)


def _pick_tile(limit, *sizes):
    t = limit
    while any(s % t for s in sizes):
        t //= 2
    return t


def _dot(a, b):
    return jnp.dot(a, b, preferred_element_type=F32)


def _dot_nt(a, b):
    return lax.dot_general(a, b, (((1,), (1,)), ((), ())), preferred_element_type=F32)


def _dot_tn(a, b):
    return lax.dot_general(a, b, (((0,), (0,)), ((), ())), preferred_element_type=F32)


def _seg_sum(x, seg):
    hi = x.astype(BF16)
    lo = (x - hi.astype(F32)).astype(BF16)
    return _dot(hi, seg) + _dot(lo, seg)


def _modulated_norm(x, gain, shift, scale):
    y = x * lax.rsqrt(jnp.mean(x * x, axis=-1, keepdims=True) + NORM_EPS)
    return (y * gain) * (1.0 + scale) + shift


def _mod_kernel(c_ref, w_ref, b_ref, o_ref):
    c = c_ref[...]
    s = c * jax.nn.sigmoid(c)
    o_ref[...] = _dot(s.astype(BF16), w_ref[...].astype(BF16)) + b_ref[...]


def _modulation(c_all, mod_w, mod_b):
    rows, d = c_all.shape
    n = mod_w.shape[1]
    tn = _pick_tile(1024, n // 6) * 1
    return pl.pallas_call(
        _mod_kernel,
        grid=(n // tn,),
        in_specs=[pl.BlockSpec((rows, d), lambda j: (0, 0)),
                  pl.BlockSpec((d, tn), lambda j: (0, j)),
                  pl.BlockSpec((1, tn), lambda j: (0, j))],
        out_specs=pl.BlockSpec((rows, tn), lambda j: (0, j)),
        out_shape=jax.ShapeDtypeStruct((rows, n), F32),
        compiler_params=_cparams("arbitrary"),
    )(c_all, mod_w, mod_b.reshape(1, n))


def _proj_kernel(x_ref, g_ref, sh_ref, sc_ref, w_ref, o_ref, h_ref):
    @pl.when(pl.program_id(1) == 0)
    def _():
        h_ref[...] = _modulated_norm(x_ref[...], g_ref[...], sh_ref[...], sc_ref[...]).astype(BF16)

    o_ref[...] = _dot(h_ref[...], w_ref[...]).astype(o_ref.dtype)


def _project(x, gain, mod_tab, which_shift, w, out_dtype, tm, tn, row_of_tile):
    m, d = x.shape
    n = w.shape[1]

    def tab_spec(which):
        return pl.BlockSpec((None, None, 1, d), lambda i, j: (row_of_tile(i), which, 0, 0))

    return pl.pallas_call(
        _proj_kernel,
        grid=(m // tm, n // tn),
        in_specs=[pl.BlockSpec((tm, d), lambda i, j: (i, 0)),
                  pl.BlockSpec((1, d), lambda i, j: (0, 0)),
                  tab_spec(which_shift), tab_spec(which_shift + 1),
                  pl.BlockSpec((d, tn), lambda i, j: (0, j))],
        out_specs=pl.BlockSpec((tm, tn), lambda i, j: (i, j)),
        out_shape=jax.ShapeDtypeStruct((m, n), out_dtype),
        scratch_shapes=[pltpu.VMEM((tm, d), BF16)],
        compiler_params=_cparams("parallel", "arbitrary"),
    )(x, gain.reshape(1, d), mod_tab, mod_tab, w)


def _seq_edges(i, tile, n_lat_rows, seq, ctx_len):
    n_lat_tiles = n_lat_rows // tile
    is_lat = i < n_lat_tiles
    per = jnp.where(is_lat, seq // tile, ctx_len // tile)
    pos = jnp.where(is_lat, i, i - n_lat_tiles) % per
    return pos == 0, pos == per - 1


def _conv_kernel(z_ref, zp_ref, zn_ref, w_ref, b_ref, lg_ref, lb_ref, o_ref, ext_ref,
                 *, tile, n_lat_rows, seq, ctx_len):
    first, last = _seq_edges(pl.program_id(0), tile, n_lat_rows, seq, ctx_len)
    cd = o_ref.shape[1]

    def glu(z):
        z = z.astype(F32)
        return z[:, :cd] * jax.nn.sigmoid(z[:, cd:])

    ext_ref[0:HALO, :] = jnp.where(first, 0.0, glu(zp_ref[...]))
    ext_ref[HALO:HALO + tile, :] = glu(z_ref[...])
    ext_ref[HALO + tile:, :] = jnp.where(last, 0.0, glu(zn_ref[...]))
    half = CONV_WIDTH // 2
    acc = jnp.zeros((tile, cd), F32)
    for k in range(CONV_WIDTH):
        acc = acc + w_ref[k:k + 1, :] * ext_ref[pl.ds(HALO - half + k, tile), :]
    u = acc + b_ref[...]
    mu = jnp.mean(u, axis=-1, keepdims=True)
    dlt = u - mu
    var = jnp.mean(dlt * dlt, axis=-1, keepdims=True)
    y = dlt * lax.rsqrt(var + LN_EPS) * lg_ref[...] + lb_ref[...]
    o_ref[...] = (y * jax.nn.sigmoid(y)).astype(o_ref.dtype)


def _halo_specs(tile, width, m):
    per = tile // HALO
    last_blk = m // HALO - 1
    return [pl.BlockSpec((tile, width), lambda i: (i, 0)),
            pl.BlockSpec((HALO, width), lambda i: (jnp.maximum(i * per - 1, 0), 0)),
            pl.BlockSpec((HALO, width), lambda i: (jnp.minimum((i + 1) * per, last_blk), 0))]


def _conv_branch(zc, dw_w, dw_b, ln_g, ln_b, tile, n_lat_rows, seq, ctx_len):
    m, width = zc.shape
    cd = width // 2
    vec = lambda a: pl.BlockSpec((1, cd), lambda i: (0, 0))
    return pl.pallas_call(
        functools.partial(_conv_kernel, tile=tile, n_lat_rows=n_lat_rows, seq=seq, ctx_len=ctx_len),
        grid=(m // tile,),
        in_specs=_halo_specs(tile, width, m) + [pl.BlockSpec((CONV_WIDTH, cd), lambda i: (0, 0)),
                                                vec(0), vec(0), vec(0)],
        out_specs=pl.BlockSpec((tile, cd), lambda i: (i, 0)),
        out_shape=jax.ShapeDtypeStruct((m, cd), BF16),
        scratch_shapes=[pltpu.VMEM((tile + 2 * HALO, cd), F32)],
        compiler_params=_cparams("parallel"),
    )(zc, zc, zc, dw_w, dw_b.reshape(1, cd), ln_g.reshape(1, cd), ln_b.reshape(1, cd))


def _rope(x, cos, sin_next, sin_prev):
    return x * cos + pltpu.roll(x, LANES - 1, 1) * sin_next + pltpu.roll(x, 1, 1) * sin_prev


def _attn_kernel(*refs, lambda_init, n_ctx, with_lat):
    if with_lat:
        (q_ref, kc_ref, vc_ref, kl_ref, vl_ref, cq_ref, snq_ref, spq_ref, ck_ref, snk_ref, spk_ref,
         lam_ref, g_ref, o_ref, ks_ref, vs_ref) = refs
    else:
        q_ref, kc_ref, vc_ref, lam_ref, g_ref, _, o_ref, ks_ref, vs_ref = refs

    @pl.when(pl.program_id(2) == 0)
    def _():
        ks_ref[0:n_ctx, :] = kc_ref[...]
        vs_ref[0:n_ctx, :] = vc_ref[...]
        if with_lat:
            k = _rope(kl_ref[...].astype(F32), ck_ref[...], snk_ref[...], spk_ref[...])
            ks_ref[n_ctx:, :] = k.astype(BF16)
            vs_ref[n_ctx:, :] = vl_ref[...]

    tq = q_ref.shape[0]
    q = q_ref[...].astype(F32)
    if with_lat:
        q = _rope(q, cq_ref[...], snq_ref[...], spq_ref[...])
    q = q * (1.0 / math.sqrt(ATT_HEAD_DIM))
    first_map = lax.broadcasted_iota(jnp.int32, q.shape, 1) < ATT_HEAD_DIM
    qs = jnp.concatenate([jnp.where(first_map, q, 0.0), jnp.where(first_map, 0.0, q)], axis=0).astype(BF16)
    s = _dot_nt(qs, ks_ref[...])
    p = jnp.exp(s - jnp.max(s, axis=-1, keepdims=True))
    l = jnp.sum(p, axis=-1, keepdims=True)
    o = _dot(p.astype(BF16), vs_ref[...]) / l
    lq = lam_ref[...]
    lam = (jnp.exp(jnp.sum(lq[0:1] * lq[1:2], axis=-1, keepdims=True))
           - jnp.exp(jnp.sum(lq[2:3] * lq[3:4], axis=-1, keepdims=True)) + lambda_init)
    a = o[:tq] - lam * o[tq:]
    y = a * lax.rsqrt(jnp.mean(a * a, axis=-1, keepdims=True) + SUBLN_EPS) * g_ref[...]
    o_ref[...] = (y * (1.0 - lambda_init)).astype(o_ref.dtype)


def _attention(qkv, rope_tabs, lam_vecs, subln_g, lambda_init, batch, seq, ctx_len, y_lat=None):
    m = qkv.shape[0]
    latent_queries = y_lat is None
    h_ = ATT_HEADS
    ctx_blk0 = batch * seq // ctx_len
    tq = _pick_tile(256, seq if latent_queries else ctx_len)
    nq = (seq if latent_queries else ctx_len) // tq
    q0 = 0 if latent_queries else batch * seq // tq
    blk = lambda rows, fn: pl.BlockSpec((rows, LANES), fn)
    in_specs = [blk(tq, lambda b, h, i: (q0 + b * nq + i, h)),
                blk(ctx_len, lambda b, h, i: (ctx_blk0 + b, h_ + h)),
                blk(ctx_len, lambda b, h, i: (ctx_blk0 + b, 2 * h_ + h))]
    args = [qkv, qkv, qkv]
    n_keys = ctx_len
    if latent_queries:
        n_keys += seq
        in_specs += [blk(seq, lambda b, h, i: (b, h_ + h)), blk(seq, lambda b, h, i: (b, 2 * h_ + h))]
        in_specs += [blk(tq, lambda b, h, i: (i, 0))] * 3 + [blk(seq, lambda b, h, i: (0, 0))] * 3
        args += [qkv, qkv] + list(rope_tabs) + list(rope_tabs)
    in_specs += [pl.BlockSpec((4, ATT_HEAD_DIM), lambda b, h, i: (0, 0)),
                 pl.BlockSpec((1, ATT_VAL_DIM), lambda b, h, i: (0, 0))]
    args += [lam_vecs, subln_g.reshape(1, ATT_VAL_DIM)]
    aliases = {}
    if not latent_queries:
        in_specs.append(pl.BlockSpec(memory_space=pl.ANY))
        args.append(y_lat)
        aliases = {len(args) - 1: 0}
    return pl.pallas_call(
        functools.partial(_attn_kernel, lambda_init=lambda_init, n_ctx=ctx_len, with_lat=latent_queries),
        grid=(batch, h_, nq),
        in_specs=in_specs,
        out_specs=blk(tq, lambda b, h, i: (q0 + b * nq + i, h)),
        out_shape=jax.ShapeDtypeStruct((m, h_ * ATT_VAL_DIM), BF16),
        scratch_shapes=[pltpu.VMEM((n_keys, LANES), BF16), pltpu.VMEM((n_keys, LANES), BF16)],
        input_output_aliases=aliases,
        compiler_params=_cparams("parallel", "parallel", "arbitrary"),
    )(*args)


def _rope_tables(seq):
    t = jnp.arange(seq, dtype=jnp.int32)
    row = (t // GRID_W).astype(F32)
    col = (t % GRID_W).astype(F32)
    n_pairs = ATT_HEAD_DIM // 4
    inv_freq = ROPE_BASE ** (-jnp.arange(n_pairs, dtype=F32) / n_pairs)
    ang = jnp.concatenate([row[:, None] * inv_freq, col[:, None] * inv_freq], axis=-1)
    cos = jnp.repeat(jnp.cos(ang), 2, axis=-1)
    sin = jnp.repeat(jnp.sin(ang), 2, axis=-1)
    even = (jnp.arange(ATT_HEAD_DIM) % 2 == 0)[None, :]
    both = lambda a: jnp.concatenate([a, a], axis=-1)
    return both(cos), both(jnp.where(even, -sin, 0.0)), both(jnp.where(even, 0.0, sin))


def _rwkv_prep_kernel(z_ref, zp_ref, zn_ref, sh_ref, w2_ref, a2_ref, g2_ref, w0_ref, a0_ref,
                      kkw_ref, ka_ref, rk_ref, seg_ref,
                      r_o, v_o, kk_o, g_o, bonus_o, lw_o, kd_o, b_o, ext_ref,
                      *, tile, n_lat_rows, seq, ctx_len):
    first, last = _seq_edges(pl.program_id(0), tile, n_lat_rows, seq, ctx_len)
    hd = RWKV_DIM
    ext_ref[0:HALO, :] = jnp.where(first, 0.0, zp_ref[...].astype(F32))
    ext_ref[HALO:HALO + tile, :] = z_ref[...].astype(F32)
    ext_ref[HALO + tile:, :] = jnp.where(last, 0.0, zn_ref[...].astype(F32))
    z = (sh_ref[0:1, :] * ext_ref[pl.ds(HALO - 1, tile), :]
         + sh_ref[1:2, :] * ext_ref[pl.ds(HALO, tile), :]
         + sh_ref[2:3, :] * ext_ref[pl.ds(HALO + 1, tile), :])
    seg = seg_ref[...]
    r = z[:, 0:hd]
    k = z[:, hd:2 * hd]
    v = z[:, 2 * hd:3 * hd]
    o = 3 * hd
    wd = z[:, o:o + 2 * DECAY_LORA]
    ad = z[:, o + 2 * DECAY_LORA:o + 2 * DECAY_LORA + 2 * ICLR_LORA]
    gd = z[:, o + 2 * DECAY_LORA + 2 * ICLR_LORA:]
    kkf = k * kkw_ref[...]
    kk = kkf * lax.rsqrt(jnp.maximum(_seg_sum(kkf * kkf, seg), 1e-24))
    g = _dot(jax.nn.sigmoid(gd).astype(BF16), g2_ref[...])
    wl = w0_ref[...] + _dot(jnp.tanh(wd).astype(BF16), w2_ref[...])
    al = a0_ref[...] + _dot(ad.astype(BF16), a2_ref[...])
    softplus = jnp.maximum(-wl, 0.0) + jnp.log(1.0 + jnp.exp(-jnp.abs(wl)))
    lw = -jnp.exp(-softplus - 0.5)
    a = jax.nn.sigmoid(al)
    r_o[...] = r
    v_o[...] = v
    kk_o[...] = kk
    g_o[...] = g
    lw_o[...] = lw
    rk = rk_ref[...]
    bonus = jnp.zeros_like(r)
    for d in range(2):
        a_d = a[:, d * hd:(d + 1) * hd]
        kd = k * (1.0 + (a_d - 1.0) * ka_ref[...])
        kd_o[:, d * hd:(d + 1) * hd] = kd
        b_o[:, d * hd:(d + 1) * hd] = kk * a_d
        bonus = bonus + _seg_sum(r * kd * rk, seg) * v
    bonus_o[...] = bonus


def _block_diag2(w):
    z = jnp.zeros_like(w[0])
    return jnp.concatenate([jnp.concatenate([w[0], z], axis=1), jnp.concatenate([z, w[1]], axis=1)], axis=0)


def _head_seg_matrix(n, head):
    idx = jnp.arange(n) // head
    return (idx[:, None] == idx[None, :]).astype(BF16)


def _rwkv_prep(zr, lp, tile, n_lat_rows, seq, ctx_len):
    m, width = zr.shape
    hd = RWKV_DIM
    full = lambda a: pl.BlockSpec(a.shape, lambda i: (0,) * a.ndim)
    consts = [lp['rwkv_shift'],
              _block_diag2(lp['rwkv_w2']).astype(BF16), _block_diag2(lp['rwkv_a2']).astype(BF16),
              lp['rwkv_g2'].astype(BF16),
              lp['rwkv_w0'].reshape(1, 2 * hd), lp['rwkv_a0'].reshape(1, 2 * hd),
              lp['rwkv_kk'].reshape(1, hd), lp['rwkv_ka'].reshape(1, hd), lp['rwkv_rk'].reshape(1, hd),
              _head_seg_matrix(hd, RWKV_HEAD_DIM)]
    one = jax.ShapeDtypeStruct((m, hd), F32)
    two = jax.ShapeDtypeStruct((m, 2 * hd), F32)
    ospec = lambda n: pl.BlockSpec((tile, n), lambda i: (i, 0))
    return pl.pallas_call(
        functools.partial(_rwkv_prep_kernel, tile=tile, n_lat_rows=n_lat_rows, seq=seq, ctx_len=ctx_len),
        grid=(m // tile,),
        in_specs=_halo_specs(tile, width, m) + [full(a) for a in consts],
        out_specs=[ospec(hd)] * 5 + [ospec(2 * hd)] * 3,
        out_shape=[one] * 5 + [two] * 3,
        scratch_shapes=[pltpu.VMEM((tile + 2 * HALO, width), F32)],
        compiler_params=_cparams("parallel"),
    )(zr, zr, zr, *consts)


def _split3(x):
    h1 = x.astype(BF16)
    r1 = x - h1.astype(F32)
    h2 = r1.astype(BF16)
    h3 = (r1 - h2.astype(F32)).astype(BF16)
    return h1, h2, h3


def _scan_kernel(r_ref, lw_ref, k_ref, v_ref, kk_ref, b_ref, y_ref, s_ref):
    c_len = r_ref.shape[0]
    fwd = pl.program_id(1) == 0

    @pl.when(pl.program_id(2) == 0)
    def _():
        s_ref[...] = jnp.zeros_like(s_ref)

    row = lax.broadcasted_iota(jnp.int32, (c_len, c_len), 0)
    col = lax.broadcasted_iota(jnp.int32, (c_len, c_len), 1)
    ahead = jnp.where(fwd, row - col, col - row)
    incl = ahead >= 0
    strict = ahead > 0
    tri = jnp.where(incl, 1.0, 0.0).astype(BF16)
    lw = lw_ref[...]
    h1, h2, h3 = _split3(lw)
    c = _dot(tri, h1) + _dot(tri, h2) + _dot(tri, h3)
    tot = jnp.sum(lw, axis=0, keepdims=True)
    e_neg = jnp.exp(-c)
    a_t = -kk_ref[...] * jnp.exp(c - lw)
    b_t = b_ref[...] * e_neg
    k_t = k_ref[...] * e_neg
    r_t = r_ref[...] * jnp.exp(c)
    dec = jnp.exp(tot - c)
    b_p = b_ref[...] * dec
    k_p = k_ref[...] * dec
    p_tot = jnp.exp(tot)
    v_all = v_ref[...]

    lane = lax.broadcasted_iota(jnp.int32, (c_len, LANES), 1)
    head0 = lane < RWKV_HEAD_DIM
    srow = lax.broadcasted_iota(jnp.int32, (LANES, LANES), 0) // RWKV_HEAD_DIM
    scol = lax.broadcasted_iota(jnp.int32, (LANES, LANES), 1) // RWKV_HEAD_DIM
    same_head = srow == scol
    eye = jnp.where(row == col, 1.0, 0.0)
    pair_masks = []
    for sh in range(int(math.log2(c_len))):
        same_pair = (row >> (sh + 1)) == (col >> (sh + 1))
        pair_masks.append(same_pair & ((row >> sh) != (col >> sh)) & strict)

    for p in range(RWKV_DIM // LANES):
        sl = slice(p * LANES, (p + 1) * LANES)
        a_p, r_p, v_p = a_t[:, sl], r_t[:, sl], v_all[:, sl]
        bb, kb, vb = b_t[:, sl].astype(BF16), k_t[:, sl].astype(BF16), v_p.astype(BF16)
        s_old = s_ref[p]
        sb = s_old.astype(BF16)
        a_s = _dot_nt(a_p.astype(BF16), sb)
        r_s = _dot_nt(r_p.astype(BF16), sb)
        u_heads, att_rb, att_rk = [], [], []
        for hh in range(2):
            keep = head0 if hh == 0 else jnp.logical_not(head0)
            lhs = jnp.concatenate([jnp.where(keep, a_p, 0.0), jnp.where(keep, r_p, 0.0)], axis=0).astype(BF16)
            ab = _dot_nt(lhs, bb)
            ak = _dot_nt(lhs, kb)
            l_k = jnp.where(strict, ab[:c_len], 0.0)
            a_ak = jnp.where(strict, ak[:c_len], 0.0)
            att_rb.append(jnp.where(incl, ab[c_len:], 0.0).astype(BF16))
            att_rk.append(jnp.where(incl, ak[c_len:], 0.0).astype(BF16))
            w = a_s + _dot(a_ak.astype(BF16), vb)
            x_inv = eye + jnp.where(pair_masks[0], l_k, 0.0)
            for pm in pair_masks[1:]:
                xb = x_inv.astype(BF16)
                t1 = _dot(jnp.where(pm, l_k, 0.0).astype(BF16), xb)
                x_inv = x_inv + _dot(xb, t1.astype(BF16))
            u_heads.append(_dot(x_inv.astype(BF16), w.astype(BF16)))
        u = jnp.where(head0, u_heads[0], u_heads[1])
        ub = u.astype(BF16)
        y0 = _dot(att_rb[0], ub) + _dot(att_rk[0], vb)
        y1 = _dot(att_rb[1], ub) + _dot(att_rk[1], vb)
        y_ref[:, sl] = r_s + jnp.where(head0, y0, y1)
        lhs_t = jnp.concatenate([ub, vb], axis=0)
        rhs = jnp.concatenate([b_p[:, sl], k_p[:, sl]], axis=0).astype(BF16)
        upd = _dot_tn(lhs_t, rhs)
        s_ref[p] = s_old * p_tot[:, sl] + jnp.where(same_head, upd, 0.0)


def _rwkv_scan(r, lw, kd, v, kk, bvec, batch, seq, ctx_len):
    m, hd = r.shape
    c_len = SCAN_CHUNK
    n_ctx, n_lat = ctx_len // c_len, seq // c_len
    ctx0 = batch * seq // c_len

    def rowblk(b, d, j):
        in_ctx = j < n_ctx
        jc = jnp.where(d == 0, j, n_ctx - 1 - j)
        jl = jnp.where(d == 0, j - n_ctx, n_lat - 1 - (j - n_ctx))
        return jnp.where(in_ctx, ctx0 + b * n_ctx + jc, b * n_lat + jl)

    shared = pl.BlockSpec((c_len, hd), lambda b, d, j: (rowblk(b, d, j), 0))
    per_dir = pl.BlockSpec((c_len, hd), lambda b, d, j: (rowblk(b, d, j), d))
    return pl.pallas_call(
        _scan_kernel,
        grid=(batch, 2, n_ctx + n_lat),
        in_specs=[shared, per_dir, per_dir, shared, shared, per_dir],
        out_specs=pl.BlockSpec((None, c_len, hd), lambda b, d, j: (d, rowblk(b, d, j), 0)),
        out_shape=jax.ShapeDtypeStruct((2, m, hd), F32),
        scratch_shapes=[pltpu.VMEM((hd // LANES, LANES, LANES), F32)],
        compiler_params=_cparams("parallel", "parallel", "arbitrary"),
    )(r, lw, kd, v, kk, bvec)


def _rwkv_out_kernel(yf_ref, yb_ref, bonus_ref, g_ref, gg_ref, gb_ref, seg_ref, o_ref):
    seg = seg_ref[...]
    ys = yf_ref[...] + yb_ref[...]
    inv_n = 1.0 / RWKV_HEAD_DIM
    mu = _seg_sum(ys, seg) * inv_n
    dlt = ys - mu
    var = _seg_sum(dlt * dlt, seg) * inv_n
    yn = dlt * lax.rsqrt(var + GN_EPS) * gg_ref[...] + gb_ref[...]
    o_ref[...] = ((yn + bonus_ref[...]) * g_ref[...]).astype(o_ref.dtype)


def _rwkv_out(ys, bonus, g, gn_g, gn_b, tile):
    _, m, hd = ys.shape
    seg = _head_seg_matrix(hd, RWKV_HEAD_DIM)
    tok = pl.BlockSpec((tile, hd), lambda i: (i, 0))
    vec = pl.BlockSpec((1, hd), lambda i: (0, 0))
    return pl.pallas_call(
        _rwkv_out_kernel,
        grid=(m // tile,),
        in_specs=[pl.BlockSpec((None, tile, hd), lambda i: (0, i, 0)),
                  pl.BlockSpec((None, tile, hd), lambda i: (1, i, 0)),
                  tok, tok, vec, vec, pl.BlockSpec((hd, hd), lambda i: (0, 0))],
        out_specs=tok,
        out_shape=jax.ShapeDtypeStruct((m, hd), BF16),
        compiler_params=_cparams("parallel"),
    )(ys, ys, bonus, g, gn_g.reshape(1, hd), gn_b.reshape(1, hd), seg)


def _merge_kernel(x_ref, yc_ref, ya_ref, yr_ref, gt_ref, pc_ref, pa_ref, pr_ref, wo_ref, gate_ref, o_ref):
    d = x_ref.shape[1]
    g = jax.nn.sigmoid(gt_ref[...].astype(F32))
    mix = (g[:, 0:d] * _dot(yc_ref[...], pc_ref[...])
           + g[:, d:2 * d] * _dot(ya_ref[...], pa_ref[...])
           + g[:, 2 * d:] * _dot(yr_ref[...], pr_ref[...]))
    o_ref[...] = x_ref[...] + gate_ref[...] * _dot(mix.astype(BF16), wo_ref[...])


def _merge(x, rows, yc, ya, yr, gates, pc, pa, pr, wo, mod_tab, tm, row_of_tile):
    d = x.shape[1]
    tok = lambda a: pl.BlockSpec((tm, a.shape[1]), lambda i: (i, 0))
    full = lambda a: pl.BlockSpec(a.shape, lambda i: (0, 0))
    return pl.pallas_call(
        _merge_kernel,
        grid=(rows // tm,),
        in_specs=[tok(x), tok(yc), tok(ya), tok(yr), tok(gates), full(pc), full(pa), full(pr), full(wo),
                  pl.BlockSpec((None, None, 1, d), lambda i: (row_of_tile(i), 2, 0, 0))],
        out_specs=tok(x),
        out_shape=jax.ShapeDtypeStruct((rows, d), F32),
        compiler_params=_cparams("parallel"),
    )(x, yc, ya, yr, gates, pc, pa, pr, wo, mod_tab)


def _mlp_kernel(*refs, final):
    if final:
        x_ref, g_ref, sh_ref, sc_ref, gate_ref, w1_ref, w2_ref, fg_ref, o_ref, h_ref, acc_ref = refs
    else:
        x_ref, g_ref, sh_ref, sc_ref, gate_ref, w1_ref, w2_ref, o_ref, h_ref, acc_ref = refs
    f = pl.program_id(1)

    @pl.when(f == 0)
    def _():
        h_ref[...] = _modulated_norm(x_ref[...], g_ref[...], sh_ref[...], sc_ref[...]).astype(BF16)
        acc_ref[...] = jnp.zeros_like(acc_ref)

    t = jnp.maximum(_dot(h_ref[...], w1_ref[...]), 0.0)
    acc_ref[...] += _dot((t * t).astype(BF16), w2_ref[...])

    @pl.when(f == pl.num_programs(1) - 1)
    def _():
        y = x_ref[...] + gate_ref[...] * acc_ref[...]
        if final:
            y = y * lax.rsqrt(jnp.mean(y * y, axis=-1, keepdims=True) + NORM_EPS) * fg_ref[...]
        o_ref[...] = y


def _mlp(x, rows, gain, mod_tab, w1, w2, final_g, tm, tf, row_of_tile):
    d = x.shape[1]
    dff = w1.shape[1]
    final = final_g is not None
    tab = lambda which: pl.BlockSpec((None, None, 1, d), lambda i, f: (row_of_tile(i), which, 0, 0))
    in_specs = [pl.BlockSpec((tm, d), lambda i, f: (i, 0)),
                pl.BlockSpec((1, d), lambda i, f: (0, 0)),
                tab(3), tab(4), tab(5),
                pl.BlockSpec((d, tf), lambda i, f: (0, f)),
                pl.BlockSpec((tf, d), lambda i, f: (f, 0))]
    args = [x, gain.reshape(1, d), mod_tab, mod_tab, mod_tab, w1, w2]
    if final:
        in_specs.append(pl.BlockSpec((1, d), lambda i, f: (0, 0)))
        args.append(final_g.reshape(1, d))
    return pl.pallas_call(
        functools.partial(_mlp_kernel, final=final),
        grid=(rows // tm, dff // tf),
        in_specs=in_specs,
        out_specs=pl.BlockSpec((tm, d), lambda i, f: (i, 0)),
        out_shape=jax.ShapeDtypeStruct((rows, d), F32),
        scratch_shapes=[pltpu.VMEM((tm, d), BF16), pltpu.VMEM((tm, d), F32)],
        compiler_params=_cparams("parallel", "arbitrary"),
    )(*args)


def kernel(x, c, ctx, c_ctx, mod_w, mod_b, norm1_g, norm2_g, w_in, conv_dw_w, conv_dw_b, conv_ln_g, conv_ln_b, p_conv, att_lq1, att_lk1, att_lq2, att_lk2, att_subln_g, p_att, rwkv_shift, rwkv_w0, rwkv_w2, rwkv_a0, rwkv_a2, rwkv_g2, rwkv_kk, rwkv_ka, rwkv_rk, rwkv_gn_g, rwkv_gn_b, p_rwkv, w_out, mlp_w1, mlp_w2, final_g):
    batch, seq, d = x.shape
    ctx_len = ctx.shape[1]
    depth = mod_w.shape[0]
    n_lat = batch * seq
    conv_w = 2 * conv_dw_w.shape[2]
    att_w = 3 * ATT_HEADS * ATT_VAL_DIM
    rw_w = 3 * RWKV_DIM + 2 * DECAY_LORA + 2 * ICLR_LORA + GATE_LORA
    assert w_in.shape[2] == conv_w + att_w + rw_w + 3 * d
    assert seq % SCAN_CHUNK == 0 and ctx_len % SCAN_CHUNK == 0 and (batch * seq) % ctx_len == 0

    tm = _pick_tile(1024, seq, batch * ctx_len)
    tmm = _pick_tile(512, seq, batch * ctx_len)
    tc = _pick_tile(256, seq, ctx_len)
    tiles_per_seq = seq // tm
    row_of_tile = lambda i: jnp.minimum(i // tiles_per_seq, batch)
    row_of_tile_m = lambda i: jnp.minimum(i // (seq // tmm), batch)

    xs = jnp.concatenate([x.reshape(n_lat, d), ctx.reshape(batch * ctx_len, d)], axis=0)
    pad = (-(batch + 1)) % 8
    c_all = jnp.concatenate([c, c_ctx[None, :], jnp.zeros((pad, d), F32)], axis=0)
    rope_tabs = _rope_tables(seq)

    for l in range(depth):
        last = l == depth - 1
        lambda_init = 0.8 - 0.6 * math.exp(-0.3 * l)
        mod = _modulation(c_all, mod_w[l], mod_b[l])
        mod_tab = mod[:batch + 1].reshape(batch + 1, 6, 1, d)

        wl = w_in[l].astype(BF16)
        o1, o2, o3 = conv_w, conv_w + att_w, conv_w + att_w + rw_w
        proj = functools.partial(_project, xs, norm1_g[l], mod_tab, 0, tm=tm, row_of_tile=row_of_tile)
        zc = proj(wl[:, :o1], BF16, tn=_pick_tile(512, o1))
        qkv = proj(wl[:, o1:o2], BF16, tn=_pick_tile(512, att_w))
        zr = proj(wl[:, o2:o3], BF16, tn=rw_w)
        gates = proj(wl[:, o3:], BF16, tn=_pick_tile(512, 3 * d))

        y_conv = _conv_branch(zc, conv_dw_w[l], conv_dw_b[l], conv_ln_g[l], conv_ln_b[l],
                              tc, n_lat, seq, ctx_len)

        lam_vecs = jnp.stack([att_lq1[l], att_lk1[l], att_lq2[l], att_lk2[l]])
        y_att = _attention(qkv, rope_tabs, lam_vecs, att_subln_g[l], lambda_init, batch, seq, ctx_len)
        if not last:
            y_att = _attention(qkv, None, lam_vecs, att_subln_g[l], lambda_init, batch, seq, ctx_len, y_att)

        lp = dict(rwkv_shift=rwkv_shift[l], rwkv_w0=rwkv_w0[l], rwkv_w2=rwkv_w2[l], rwkv_a0=rwkv_a0[l],
                  rwkv_a2=rwkv_a2[l], rwkv_g2=rwkv_g2[l], rwkv_kk=rwkv_kk[l], rwkv_ka=rwkv_ka[l],
                  rwkv_rk=rwkv_rk[l])
        r, v, kk, g, bonus, lw, kd, bvec = _rwkv_prep(zr, lp, tc, n_lat, seq, ctx_len)
        ys = _rwkv_scan(r, lw, kd, v, kk, bvec, batch, seq, ctx_len)
        y_rw = _rwkv_out(ys, bonus, g, rwkv_gn_g[l], rwkv_gn_b[l], tc)

        rows = n_lat if last else xs.shape[0]
        xs = _merge(xs, rows, y_conv, y_att, y_rw, gates, p_conv[l].astype(BF16), p_att[l].astype(BF16),
                    p_rwkv[l].astype(BF16), w_out[l].astype(BF16), mod_tab, tmm, row_of_tile_m)
        xs = _mlp(xs, rows, norm2_g[l], mod_tab, mlp_w1[l].astype(BF16), mlp_w2[l].astype(BF16),
                  final_g if last else None, tm, _pick_tile(512, mlp_w1.shape[2]), row_of_tile)
    return xs.reshape(batch, seq, d)
```

```python
import functools
import math

import jax
import jax.numpy as jnp
from jax import lax
from jax.experimental import pallas as pl
from jax.experimental.pallas import tpu as pltpu

F32 = jnp.float32
BF16 = jnp.bfloat16

GRID_W = 64
NORM_EPS = 1e-6
CONV_WIDTH = 31
LN_EPS = 1e-5
ATT_HEADS = 4
ATT_HEAD_DIM = 64
ATT_VAL_DIM = 2 * ATT_HEAD_DIM
ROPE_BASE = 10000.0
SUBLN_EPS = 1e-5
RWKV_HEADS = 8
RWKV_HEAD_DIM = 64
RWKV_DIM = RWKV_HEADS * RWKV_HEAD_DIM
DECAY_LORA = 64
ICLR_LORA = 64
GATE_LORA = 128
GN_EPS = 64e-5

LANES = 128
HALO = 16
SCAN_CHUNK = 64
VMEM_LIMIT = 56 << 20


def _cparams(*sem):
    return pltpu.CompilerParams(dimension_semantics=sem, vmem_limit_bytes=VMEM_LIMIT)


def _pick_tile(limit, *sizes):
    t = limit
    while any(s % t for s in sizes):
        t //= 2
    return t


def _dot(a, b):
    return jnp.dot(a, b, preferred_element_type=F32)


def _dot_nt(a, b):
    return lax.dot_general(a, b, (((1,), (1,)), ((), ())), preferred_element_type=F32)


def _dot_tn(a, b):
    return lax.dot_general(a, b, (((0,), (0,)), ((), ())), preferred_element_type=F32)


def _seg_sum(x, seg):
    hi = x.astype(BF16)
    lo = (x - hi.astype(F32)).astype(BF16)
    return _dot(hi, seg) + _dot(lo, seg)


def _modulated_norm(x, gain, shift, scale):
    y = x * lax.rsqrt(jnp.mean(x * x, axis=-1, keepdims=True) + NORM_EPS)
    return (y * gain) * (1.0 + scale) + shift


def _mod_kernel(c_ref, w_ref, b_ref, o_ref):
    c = c_ref[...]
    s = c * jax.nn.sigmoid(c)
    o_ref[...] = _dot(s.astype(BF16), w_ref[...].astype(BF16)) + b_ref[...]


def _modulation(c_all, mod_w, mod_b):
    rows, d = c_all.shape
    n = mod_w.shape[1]
    tn = _pick_tile(1024, n // 6)
    return pl.pallas_call(
        _mod_kernel,
        grid=(n // tn,),
        in_specs=[pl.BlockSpec((rows, d), lambda j: (0, 0)),
                  pl.BlockSpec((d, tn), lambda j: (0, j)),
                  pl.BlockSpec((1, tn), lambda j: (0, j))],
        out_specs=pl.BlockSpec((rows, tn), lambda j: (0, j)),
        out_shape=jax.ShapeDtypeStruct((rows, n), F32),
        compiler_params=_cparams("arbitrary"),
        name="adaln_modulation",
    )(c_all, mod_w, mod_b.reshape(1, n))


def _proj_kernel(x_ref, g_ref, sh_ref, sc_ref, w_ref, o_ref, h_ref):
    @pl.when(pl.program_id(1) == 0)
    def _():
        h_ref[...] = _modulated_norm(x_ref[...], g_ref[...], sh_ref[...], sc_ref[...]).astype(BF16)

    o_ref[...] = _dot(h_ref[...], w_ref[...]).astype(o_ref.dtype)


def _project(x, gain, mod_tab, which_shift, w, out_dtype, tm, tn, row_of_tile, name):
    m, d = x.shape
    n = w.shape[1]

    def tab_spec(which):
        return pl.BlockSpec((None, None, 1, d), lambda i, j: (row_of_tile(i), which, 0, 0))

    return pl.pallas_call(
        _proj_kernel,
        grid=(m // tm, n // tn),
        in_specs=[pl.BlockSpec((tm, d), lambda i, j: (i, 0)),
                  pl.BlockSpec((1, d), lambda i, j: (0, 0)),
                  tab_spec(which_shift), tab_spec(which_shift + 1),
                  pl.BlockSpec((d, tn), lambda i, j: (0, j))],
        out_specs=pl.BlockSpec((tm, tn), lambda i, j: (i, j)),
        out_shape=jax.ShapeDtypeStruct((m, n), out_dtype),
        scratch_shapes=[pltpu.VMEM((tm, d), BF16)],
        compiler_params=_cparams("parallel", "arbitrary"),
        name=name,
    )(x, gain.reshape(1, d), mod_tab, mod_tab, w)


def _seq_edges(i, tile, n_lat_rows, seq, ctx_len):
    n_lat_tiles = n_lat_rows // tile
    is_lat = i < n_lat_tiles
    per = jnp.where(is_lat, seq // tile, ctx_len // tile)
    pos = jnp.where(is_lat, i, i - n_lat_tiles) % per
    return pos == 0, pos == per - 1


def _conv_kernel(z_ref, zp_ref, zn_ref, w_ref, b_ref, lg_ref, lb_ref, o_ref, ext_ref,
                 *, tile, n_lat_rows, seq, ctx_len):
    first, last = _seq_edges(pl.program_id(0), tile, n_lat_rows, seq, ctx_len)
    cd = o_ref.shape[1]

    def glu(z):
        z = z.astype(F32)
        return z[:, :cd] * jax.nn.sigmoid(z[:, cd:])

    ext_ref[0:HALO, :] = jnp.where(first, 0.0, glu(zp_ref[...]))
    ext_ref[HALO:HALO + tile, :] = glu(z_ref[...])
    ext_ref[HALO + tile:, :] = jnp.where(last, 0.0, glu(zn_ref[...]))
    half = CONV_WIDTH // 2
    acc = jnp.zeros((tile, cd), F32)
    for k in range(CONV_WIDTH):
        acc = acc + w_ref[k:k + 1, :] * ext_ref[pl.ds(HALO - half + k, tile), :]
    u = acc + b_ref[...]
    mu = jnp.mean(u, axis=-1, keepdims=True)
    dlt = u - mu
    var = jnp.mean(dlt * dlt, axis=-1, keepdims=True)
    y = dlt * lax.rsqrt(var + LN_EPS) * lg_ref[...] + lb_ref[...]
    o_ref[...] = (y * jax.nn.sigmoid(y)).astype(o_ref.dtype)


def _halo_specs(tile, width, m):
    per = tile // HALO
    last_blk = m // HALO - 1
    return [pl.BlockSpec((tile, width), lambda i: (i, 0)),
            pl.BlockSpec((HALO, width), lambda i: (jnp.maximum(i * per - 1, 0), 0)),
            pl.BlockSpec((HALO, width), lambda i: (jnp.minimum((i + 1) * per, last_blk), 0))]


def _conv_branch(zc, dw_w, dw_b, ln_g, ln_b, tile, n_lat_rows, seq, ctx_len):
    m, width = zc.shape
    cd = width // 2
    vec = pl.BlockSpec((1, cd), lambda i: (0, 0))
    return pl.pallas_call(
        functools.partial(_conv_kernel, tile=tile, n_lat_rows=n_lat_rows, seq=seq, ctx_len=ctx_len),
        grid=(m // tile,),
        in_specs=_halo_specs(tile, width, m) + [pl.BlockSpec((CONV_WIDTH, cd), lambda i: (0, 0)),
                                                vec, vec, vec],
        out_specs=pl.BlockSpec((tile, cd), lambda i: (i, 0)),
        out_shape=jax.ShapeDtypeStruct((m, cd), BF16),
        scratch_shapes=[pltpu.VMEM((tile + 2 * HALO, cd), F32)],
        compiler_params=_cparams("parallel"),
        name="conv_branch",
    )(zc, zc, zc, dw_w, dw_b.reshape(1, cd), ln_g.reshape(1, cd), ln_b.reshape(1, cd))


def _rope(x, cos, sin_next, sin_prev):
    return x * cos + pltpu.roll(x, LANES - 1, 1) * sin_next + pltpu.roll(x, 1, 1) * sin_prev


def _attn_kernel(*refs, lambda_init, n_ctx, with_lat):
    if with_lat:
        (q_ref, kc_ref, vc_ref, kl_ref, vl_ref, cq_ref, snq_ref, spq_ref, ck_ref, snk_ref, spk_ref,
         lam_ref, g_ref, o_ref, ks_ref, vs_ref) = refs
    else:
        q_ref, kc_ref, vc_ref, lam_ref, g_ref, _, o_ref, ks_ref, vs_ref = refs
    dv = ATT_VAL_DIM

    @pl.when(pl.program_id(2) == 0)
    def _():
        vs_ref[:, dv:] = jnp.ones((vs_ref.shape[0], dv), BF16)
        ks_ref[0:n_ctx, :] = kc_ref[...]
        vs_ref[0:n_ctx, 0:dv] = vc_ref[...]
        if with_lat:
            k = _rope(kl_ref[...].astype(F32), ck_ref[...], snk_ref[...], spk_ref[...])
            ks_ref[n_ctx:, :] = k.astype(BF16)
            vs_ref[n_ctx:, 0:dv] = vl_ref[...]

    tq = q_ref.shape[0]
    q = q_ref[...].astype(F32)
    if with_lat:
        q = _rope(q, cq_ref[...], snq_ref[...], spq_ref[...])
    q = q * (math.log2(math.e) / math.sqrt(ATT_HEAD_DIM))
    first_map = lax.broadcasted_iota(jnp.int32, q.shape, 1) < ATT_HEAD_DIM
    qs = jnp.concatenate([jnp.where(first_map, q, 0.0), jnp.where(first_map, 0.0, q)], axis=0).astype(BF16)
    s = _dot_nt(qs, ks_ref[...])
    p = jnp.exp2((s - jnp.max(s, axis=-1, keepdims=True)).astype(BF16))
    ol = _dot(p, vs_ref[...])
    o = ol[:, 0:dv] / ol[:, dv:]
    lq = lam_ref[...]
    lam = (jnp.exp(jnp.sum(lq[0:1] * lq[1:2], axis=-1, keepdims=True))
           - jnp.exp(jnp.sum(lq[2:3] * lq[3:4], axis=-1, keepdims=True)) + lambda_init)
    a = o[:tq] - lam * o[tq:]
    y = a * lax.rsqrt(jnp.mean(a * a, axis=-1, keepdims=True) + SUBLN_EPS) * g_ref[...]
    o_ref[...] = (y * (1.0 - lambda_init)).astype(o_ref.dtype)


def _attention(qkv, rope_tabs, lam_vecs, subln_g, lambda_init, batch, seq, ctx_len, y_lat=None):
    m = qkv.shape[0]
    latent_queries = y_lat is None
    h_ = ATT_HEADS
    ctx_blk0 = batch * seq // ctx_len
    tq = _pick_tile(256, seq if latent_queries else ctx_len)
    nq = (seq if latent_queries else ctx_len) // tq
    q0 = 0 if latent_queries else batch * seq // tq
    blk = lambda rows, fn: pl.BlockSpec((rows, LANES), fn)
    in_specs = [blk(tq, lambda b, h, i: (q0 + b * nq + i, h)),
                blk(ctx_len, lambda b, h, i: (ctx_blk0 + b, h_ + h)),
                blk(ctx_len, lambda b, h, i: (ctx_blk0 + b, 2 * h_ + h))]
    args = [qkv, qkv, qkv]
    n_keys = ctx_len
    if latent_queries:
        n_keys += seq
        in_specs += [blk(seq, lambda b, h, i: (b, h_ + h)), blk(seq, lambda b, h, i: (b, 2 * h_ + h))]
        in_specs += [blk(tq, lambda b, h, i: (i, 0))] * 3 + [blk(seq, lambda b, h, i: (0, 0))] * 3
        args += [qkv, qkv] + list(rope_tabs) + list(rope_tabs)
    in_specs += [pl.BlockSpec((4, ATT_HEAD_DIM), lambda b, h, i: (0, 0)),
                 pl.BlockSpec((1, ATT_VAL_DIM), lambda b, h, i: (0, 0))]
    args += [lam_vecs, subln_g.reshape(1, ATT_VAL_DIM)]
    aliases = {}
    if not latent_queries:
        in_specs.append(pl.BlockSpec(memory_space=pl.ANY))
        args.append(y_lat)
        aliases = {len(args) - 1: 0}
    return pl.pallas_call(
        functools.partial(_attn_kernel, lambda_init=lambda_init, n_ctx=ctx_len, with_lat=latent_queries),
        grid=(batch, h_, nq),
        in_specs=in_specs,
        out_specs=blk(tq, lambda b, h, i: (q0 + b * nq + i, h)),
        out_shape=jax.ShapeDtypeStruct((m, h_ * ATT_VAL_DIM), BF16),
        scratch_shapes=[pltpu.VMEM((n_keys, LANES), BF16), pltpu.VMEM((n_keys, 2 * ATT_VAL_DIM), BF16)],
        input_output_aliases=aliases,
        compiler_params=_cparams("parallel", "parallel", "arbitrary"),
        name="diff_attention_latent" if latent_queries else "diff_attention_context",
    )(*args)


def _rope_tables(seq):
    t = jnp.arange(seq, dtype=jnp.int32)
    row = (t // GRID_W).astype(F32)
    col = (t % GRID_W).astype(F32)
    n_pairs = ATT_HEAD_DIM // 4
    inv_freq = ROPE_BASE ** (-jnp.arange(n_pairs, dtype=F32) / n_pairs)
    ang = jnp.concatenate([row[:, None] * inv_freq, col[:, None] * inv_freq], axis=-1)
    cos = jnp.repeat(jnp.cos(ang), 2, axis=-1)
    sin = jnp.repeat(jnp.sin(ang), 2, axis=-1)
    even = (jnp.arange(ATT_HEAD_DIM) % 2 == 0)[None, :]
    both = lambda a: jnp.concatenate([a, a], axis=-1)
    return both(cos), both(jnp.where(even, -sin, 0.0)), both(jnp.where(even, 0.0, sin))


def _rwkv_prep_kernel(z_ref, zp_ref, zn_ref, sh_ref, w2_ref, a2_ref, g2_ref, w0_ref, a0_ref,
                      kkw_ref, ka_ref, rk_ref, seg_ref,
                      r_o, v_o, kk_o, g_o, bonus_o, lw_o, kd_o, b_o, ext_ref,
                      *, tile, n_lat_rows, seq, ctx_len):
    first, last = _seq_edges(pl.program_id(0), tile, n_lat_rows, seq, ctx_len)
    hd = RWKV_DIM
    ext_ref[0:HALO, :] = jnp.where(first, 0.0, zp_ref[...].astype(F32))
    ext_ref[HALO:HALO + tile, :] = z_ref[...].astype(F32)
    ext_ref[HALO + tile:, :] = jnp.where(last, 0.0, zn_ref[...].astype(F32))
    z = (sh_ref[0:1, :] * ext_ref[pl.ds(HALO - 1, tile), :]
         + sh_ref[1:2, :] * ext_ref[pl.ds(HALO, tile), :]
         + sh_ref[2:3, :] * ext_ref[pl.ds(HALO + 1, tile), :])
    seg = seg_ref[...]
    r = z[:, 0:hd]
    k = z[:, hd:2 * hd]
    v = z[:, 2 * hd:3 * hd]
    o = 3 * hd
    wd = z[:, o:o + 2 * DECAY_LORA]
    ad = z[:, o + 2 * DECAY_LORA:o + 2 * DECAY_LORA + 2 * ICLR_LORA]
    gd = z[:, o + 2 * DECAY_LORA + 2 * ICLR_LORA:]
    kkf = k * kkw_ref[...]
    kk = kkf * lax.rsqrt(jnp.maximum(_seg_sum(kkf * kkf, seg), 1e-24))
    g = _dot(jax.nn.sigmoid(gd).astype(BF16), g2_ref[...])
    wl = w0_ref[...] + _dot(jnp.tanh(wd).astype(BF16), w2_ref[...])
    al = a0_ref[...] + _dot(ad.astype(BF16), a2_ref[...])
    softplus = jnp.maximum(-wl, 0.0) + jnp.log(1.0 + jnp.exp(-jnp.abs(wl)))
    lw = -jnp.exp(-softplus - 0.5)
    a = jax.nn.sigmoid(al)
    r_o[...] = r
    v_o[...] = v
    kk_o[...] = kk
    g_o[...] = g
    lw_o[...] = lw
    rk = rk_ref[...]
    bonus = jnp.zeros_like(r)
    for d in range(2):
        a_d = a[:, d * hd:(d + 1) * hd]
        kd = k * (1.0 + (a_d - 1.0) * ka_ref[...])
        kd_o[:, d * hd:(d + 1) * hd] = kd
        b_o[:, d * hd:(d + 1) * hd] = kk * a_d
        bonus = bonus + _seg_sum(r * kd * rk, seg) * v
    bonus_o[...] = bonus


def _block_diag2(w):
    z = jnp.zeros_like(w[0])
    return jnp.concatenate([jnp.concatenate([w[0], z], axis=1), jnp.concatenate([z, w[1]], axis=1)], axis=0)


def _head_seg_matrix(n, head):
    idx = jnp.arange(n) // head
    return (idx[:, None] == idx[None, :]).astype(BF16)


def _rwkv_prep(zr, lp, tile, n_lat_rows, seq, ctx_len):
    m, width = zr.shape
    hd = RWKV_DIM
    full = lambda a: pl.BlockSpec(a.shape, lambda i: (0,) * a.ndim)
    consts = [lp['rwkv_shift'],
              _block_diag2(lp['rwkv_w2']).astype(BF16), _block_diag2(lp['rwkv_a2']).astype(BF16),
              lp['rwkv_g2'].astype(BF16),
              lp['rwkv_w0'].reshape(1, 2 * hd), lp['rwkv_a0'].reshape(1, 2 * hd),
              lp['rwkv_kk'].reshape(1, hd), lp['rwkv_ka'].reshape(1, hd), lp['rwkv_rk'].reshape(1, hd),
              _head_seg_matrix(hd, RWKV_HEAD_DIM)]
    one = jax.ShapeDtypeStruct((m, hd), F32)
    two = jax.ShapeDtypeStruct((m, 2 * hd), F32)
    ospec = lambda n: pl.BlockSpec((tile, n), lambda i: (i, 0))
    return pl.pallas_call(
        functools.partial(_rwkv_prep_kernel, tile=tile, n_lat_rows=n_lat_rows, seq=seq, ctx_len=ctx_len),
        grid=(m // tile,),
        in_specs=_halo_specs(tile, width, m) + [full(a) for a in consts],
        out_specs=[ospec(hd)] * 5 + [ospec(2 * hd)] * 3,
        out_shape=[one] * 5 + [two] * 3,
        scratch_shapes=[pltpu.VMEM((tile + 2 * HALO, width), F32)],
        compiler_params=_cparams("parallel"),
        name="rwkv_prep",
    )(zr, zr, zr, *consts)


def _split3(x):
    h1 = x.astype(BF16)
    r1 = x - h1.astype(F32)
    h2 = r1.astype(BF16)
    h3 = (r1 - h2.astype(F32)).astype(BF16)
    return h1, h2, h3


def _scan_kernel(*refs):
    ins = (refs[0:6], refs[6:12])
    y_refs = refs[12:14]
    s_ref = refs[14]
    c_len = refs[0].shape[0]
    n_pairs = RWKV_DIM // LANES

    @pl.when(pl.program_id(1) == 0)
    def _():
        s_ref[...] = jnp.zeros_like(s_ref)

    row = lax.broadcasted_iota(jnp.int32, (c_len, c_len), 0)
    col = lax.broadcasted_iota(jnp.int32, (c_len, c_len), 1)
    row2 = lax.broadcasted_iota(jnp.int32, (c_len, 2 * c_len), 0)
    col2 = lax.broadcasted_iota(jnp.int32, (c_len, 2 * c_len), 1)
    second_half = col2 >= c_len
    col2 = jnp.where(second_half, col2 - c_len, col2)
    eye = jnp.where(row == col, 1.0, 0.0)
    lane = lax.broadcasted_iota(jnp.int32, (c_len, LANES), 1)
    head0 = lane < RWKV_HEAD_DIM
    srow = lax.broadcasted_iota(jnp.int32, (LANES, LANES), 0) // RWKV_HEAD_DIM
    scol = lax.broadcasted_iota(jnp.int32, (LANES, LANES), 1) // RWKV_HEAD_DIM
    same_head = srow == scol

    dirs = []
    for d in range(2):
        r_ref, lw_ref, k_ref, v_ref, kk_ref, b_ref = ins[d]
        ahead = (row - col) if d == 0 else (col - row)
        ahead2 = (row2 - col2) if d == 0 else (col2 - row2)
        strict = ahead > 0
        tri = jnp.where(ahead >= 0, 1.0, 0.0).astype(BF16)
        pair_masks = []
        for sh in range(int(math.log2(c_len))):
            same_pair = (row >> (sh + 1)) == (col >> (sh + 1))
            pair_masks.append(same_pair & ((row >> sh) != (col >> sh)) & strict)
        lw = lw_ref[...]
        h1, h2, h3 = _split3(lw)
        c = _dot(tri, h1) + _dot(tri, h2) + _dot(tri, h3)
        tot = jnp.sum(lw, axis=0, keepdims=True)
        e_neg = jnp.exp(-c)
        dec = jnp.exp(tot - c)
        dirs.append(dict(
            strict=strict, pair_masks=pair_masks,
            mask_ak=second_half & (ahead2 > 0),
            mask_y=ahead2 >= 0,
            a_t=-kk_ref[...] * jnp.exp(c - lw), r_t=r_ref[...] * jnp.exp(c),
            b_t=(b_ref[...] * e_neg).astype(BF16), k_t=(k_ref[...] * e_neg).astype(BF16),
            b_p=(b_ref[...] * dec).astype(BF16), k_p=(k_ref[...] * dec).astype(BF16),
            p_tot=jnp.exp(tot), v=v_ref[...].astype(BF16)))

    groups = [(d, p) for d in range(2) for p in range(n_pairs)]
    chains = [(d, p, hh) for (d, p) in groups for hh in range(2)]

    a_s, r_s, s_old = {}, {}, {}
    for d, p in groups:
        sl = slice(p * LANES, (p + 1) * LANES)
        s_old[d, p] = s_ref[d, p]
        sb = s_old[d, p].astype(BF16)
        a_s[d, p] = _dot_nt(dirs[d]['a_t'][:, sl].astype(BF16), sb)
        r_s[d, p] = _dot_nt(dirs[d]['r_t'][:, sl].astype(BF16), sb)
    scores = {}
    for d, p, hh in chains:
        sl = slice(p * LANES, (p + 1) * LANES)
        keep = head0 if hh == 0 else jnp.logical_not(head0)
        lhs = jnp.concatenate([jnp.where(keep, dirs[d]['a_t'][:, sl], 0.0),
                               jnp.where(keep, dirs[d]['r_t'][:, sl], 0.0)], axis=0).astype(BF16)
        rhs = jnp.concatenate([dirs[d]['b_t'][:, sl], dirs[d]['k_t'][:, sl]], axis=0)
        scores[d, p, hh] = _dot_nt(lhs, rhs)
    vv = {}
    for d, p in groups:
        v_p = dirs[d]['v'][:, p * LANES:(p + 1) * LANES]
        vv[d, p] = jnp.concatenate([v_p, v_p], axis=0)
    w, l_mat, x_inv = {}, {}, {}
    for ch in chains:
        d, p, hh = ch
        top = scores[ch][:c_len]
        w[ch] = a_s[d, p] + _dot(jnp.where(dirs[d]['mask_ak'], top, 0.0).astype(BF16), vv[d, p])
        l_mat[ch] = top[:, :c_len]
        x_inv[ch] = eye + jnp.where(dirs[d]['pair_masks'][0], l_mat[ch], 0.0)

    for level in range(1, int(math.log2(c_len))):
        t1 = {}
        for ch in chains:
            pm = dirs[ch[0]]['pair_masks'][level]
            t1[ch] = _dot(jnp.where(pm, l_mat[ch], 0.0).astype(BF16), x_inv[ch].astype(BF16))
        for ch in chains:
            x_inv[ch] = x_inv[ch] + _dot(x_inv[ch].astype(BF16), t1[ch].astype(BF16))
    u_h = {ch: _dot(x_inv[ch].astype(BF16), w[ch].astype(BF16)) for ch in chains}

    for d, p in groups:
        sl = slice(p * LANES, (p + 1) * LANES)
        ub = jnp.where(head0, u_h[d, p, 0], u_h[d, p, 1]).astype(BF16)
        uv = jnp.concatenate([ub, vv[d, p][:c_len]], axis=0)
        ys = [_dot(jnp.where(dirs[d]['mask_y'], scores[d, p, hh][c_len:], 0.0).astype(BF16), uv)
              for hh in range(2)]
        y_refs[d][:, sl] = r_s[d, p] + jnp.where(head0, ys[0], ys[1])
        bk = jnp.concatenate([dirs[d]['b_p'][:, sl], dirs[d]['k_p'][:, sl]], axis=0)
        upd = _dot_tn(uv, bk)
        s_ref[d, p] = s_old[d, p] * dirs[d]['p_tot'][:, sl] + jnp.where(same_head, upd, 0.0)


def _rwkv_scan(r, lw, kd, v, kk, bvec, batch, seq, ctx_len):
    m, hd = r.shape
    c_len = SCAN_CHUNK
    n_ctx, n_lat = ctx_len // c_len, seq // c_len
    ctx0 = batch * seq // c_len

    def rowblk(d):
        def fn(b, j):
            jc = j if d == 0 else n_ctx - 1 - j
            jl = (j - n_ctx) if d == 0 else n_lat - 1 - (j - n_ctx)
            return jnp.where(j < n_ctx, ctx0 + b * n_ctx + jc, b * n_lat + jl)
        return fn

    in_specs, args = [], []
    for d in range(2):
        shared = pl.BlockSpec((c_len, hd), lambda b, j, f=rowblk(d): (f(b, j), 0))
        per_dir = pl.BlockSpec((c_len, hd), lambda b, j, f=rowblk(d), d=d: (f(b, j), d))
        in_specs += [shared, per_dir, per_dir, shared, shared, per_dir]
        args += [r, lw, kd, v, kk, bvec]
    out = jax.ShapeDtypeStruct((m, hd), F32)
    return pl.pallas_call(
        _scan_kernel,
        grid=(batch, n_ctx + n_lat),
        in_specs=in_specs,
        out_specs=[pl.BlockSpec((c_len, hd), lambda b, j, f=rowblk(d): (f(b, j), 0)) for d in range(2)],
        out_shape=[out, out],
        scratch_shapes=[pltpu.VMEM((2, hd // LANES, LANES, LANES), F32)],
        compiler_params=_cparams("parallel", "arbitrary"),
        name="rwkv_scan",
    )(*args)


def _rwkv_out_kernel(yf_ref, yb_ref, bonus_ref, g_ref, gg_ref, gb_ref, seg_ref, o_ref):
    seg = seg_ref[...]
    ys = yf_ref[...] + yb_ref[...]
    inv_n = 1.0 / RWKV_HEAD_DIM
    mu = _seg_sum(ys, seg) * inv_n
    dlt = ys - mu
    var = _seg_sum(dlt * dlt, seg) * inv_n
    yn = dlt * lax.rsqrt(var + GN_EPS) * gg_ref[...] + gb_ref[...]
    o_ref[...] = ((yn + bonus_ref[...]) * g_ref[...]).astype(o_ref.dtype)


def _rwkv_out(yf, yb, bonus, g, gn_g, gn_b, tile):
    m, hd = yf.shape
    seg = _head_seg_matrix(hd, RWKV_HEAD_DIM)
    tok = pl.BlockSpec((tile, hd), lambda i: (i, 0))
    vec = pl.BlockSpec((1, hd), lambda i: (0, 0))
    return pl.pallas_call(
        _rwkv_out_kernel,
        grid=(m // tile,),
        in_specs=[tok, tok, tok, tok, vec, vec, pl.BlockSpec((hd, hd), lambda i: (0, 0))],
        out_specs=tok,
        out_shape=jax.ShapeDtypeStruct((m, hd), BF16),
        compiler_params=_cparams("parallel"),
        name="rwkv_out",
    )(yf, yb, bonus, g, gn_g.reshape(1, hd), gn_b.reshape(1, hd), seg)


def _merge_kernel(x_ref, yc_ref, ya_ref, yr_ref, gt_ref, pc_ref, pa_ref, pr_ref, wo_ref, gate_ref, o_ref):
    d = x_ref.shape[1]
    g = jax.nn.sigmoid(gt_ref[...].astype(F32))
    mix = (g[:, 0:d] * _dot(yc_ref[...], pc_ref[...])
           + g[:, d:2 * d] * _dot(ya_ref[...], pa_ref[...])
           + g[:, 2 * d:] * _dot(yr_ref[...], pr_ref[...]))
    o_ref[...] = x_ref[...] + gate_ref[...] * _dot(mix.astype(BF16), wo_ref[...])


def _merge(x, rows, yc, ya, yr, gates, pc, pa, pr, wo, mod_tab, tm, row_of_tile):
    d = x.shape[1]
    tok = lambda a: pl.BlockSpec((tm, a.shape[1]), lambda i: (i, 0))
    full = lambda a: pl.BlockSpec(a.shape, lambda i: (0, 0))
    return pl.pallas_call(
        _merge_kernel,
        grid=(rows // tm,),
        in_specs=[tok(x), tok(yc), tok(ya), tok(yr), tok(gates), full(pc), full(pa), full(pr), full(wo),
                  pl.BlockSpec((None, None, 1, d), lambda i: (row_of_tile(i), 2, 0, 0))],
        out_specs=tok(x),
        out_shape=jax.ShapeDtypeStruct((rows, d), F32),
        compiler_params=_cparams("parallel"),
        name="merge_out_proj",
    )(x, yc, ya, yr, gates, pc, pa, pr, wo, mod_tab)


def _mlp_kernel(*refs, final):
    if final:
        x_ref, g_ref, sh_ref, sc_ref, gate_ref, w1_ref, w2_ref, fg_ref, o_ref, h_ref, acc_ref = refs
    else:
        x_ref, g_ref, sh_ref, sc_ref, gate_ref, w1_ref, w2_ref, o_ref, h_ref, acc_ref = refs
    f = pl.program_id(1)

    @pl.when(f == 0)
    def _():
        h_ref[...] = _modulated_norm(x_ref[...], g_ref[...], sh_ref[...], sc_ref[...]).astype(BF16)
        acc_ref[...] = jnp.zeros_like(acc_ref)

    t = jnp.maximum(_dot(h_ref[...], w1_ref[...]), 0.0)
    acc_ref[...] += _dot((t * t).astype(BF16), w2_ref[...])

    @pl.when(f == pl.num_programs(1) - 1)
    def _():
        y = x_ref[...] + gate_ref[...] * acc_ref[...]
        if final:
            y = y * lax.rsqrt(jnp.mean(y * y, axis=-1, keepdims=True) + NORM_EPS) * fg_ref[...]
        o_ref[...] = y


def _mlp(x, rows, gain, mod_tab, w1, w2, final_g, tm, tf, row_of_tile):
    d = x.shape[1]
    dff = w1.shape[1]
    final = final_g is not None
    tab = lambda which: pl.BlockSpec((None, None, 1, d), lambda i, f: (row_of_tile(i), which, 0, 0))
    in_specs = [pl.BlockSpec((tm, d), lambda i, f: (i, 0)),
                pl.BlockSpec((1, d), lambda i, f: (0, 0)),
                tab(3), tab(4), tab(5),
                pl.BlockSpec((d, tf), lambda i, f: (0, f)),
                pl.BlockSpec((tf, d), lambda i, f: (f, 0))]
    args = [x, gain.reshape(1, d), mod_tab, mod_tab, mod_tab, w1, w2]
    if final:
        in_specs.append(pl.BlockSpec((1, d), lambda i, f: (0, 0)))
        args.append(final_g.reshape(1, d))
    return pl.pallas_call(
        functools.partial(_mlp_kernel, final=final),
        grid=(rows // tm, dff // tf),
        in_specs=in_specs,
        out_specs=pl.BlockSpec((tm, d), lambda i, f: (i, 0)),
        out_shape=jax.ShapeDtypeStruct((rows, d), F32),
        scratch_shapes=[pltpu.VMEM((tm, d), BF16), pltpu.VMEM((tm, d), F32)],
        compiler_params=_cparams("parallel", "arbitrary"),
        name="mlp_final" if final else "mlp",
    )(*args)


def kernel(x, c, ctx, c_ctx, mod_w, mod_b, norm1_g, norm2_g, w_in, conv_dw_w, conv_dw_b, conv_ln_g, conv_ln_b, p_conv, att_lq1, att_lk1, att_lq2, att_lk2, att_subln_g, p_att, rwkv_shift, rwkv_w0, rwkv_w2, rwkv_a0, rwkv_a2, rwkv_g2, rwkv_kk, rwkv_ka, rwkv_rk, rwkv_gn_g, rwkv_gn_b, p_rwkv, w_out, mlp_w1, mlp_w2, final_g):
    batch, seq, d = x.shape
    ctx_len = ctx.shape[1]
    depth = mod_w.shape[0]
    n_lat = batch * seq
    conv_w = 2 * conv_dw_w.shape[2]
    att_w = 3 * ATT_HEADS * ATT_VAL_DIM
    rw_w = 3 * RWKV_DIM + 2 * DECAY_LORA + 2 * ICLR_LORA + GATE_LORA
    assert w_in.shape[2] == conv_w + att_w + rw_w + 3 * d
    assert seq % SCAN_CHUNK == 0 and ctx_len % SCAN_CHUNK == 0 and (batch * seq) % ctx_len == 0

    tm = _pick_tile(1024, seq, batch * ctx_len)
    tmm = _pick_tile(512, seq, batch * ctx_len)
    tc = _pick_tile(256, seq, ctx_len)
    tiles_per_seq = seq // tm
    row_of_tile = lambda i: jnp.minimum(i // tiles_per_seq, batch)
    row_of_tile_m = lambda i: jnp.minimum(i // (seq // tmm), batch)

    xs = jnp.concatenate([x.reshape(n_lat, d), ctx.reshape(batch * ctx_len, d)], axis=0)
    pad = (-(batch + 1)) % 8
    c_all = jnp.concatenate([c, c_ctx[None, :], jnp.zeros((pad, d), F32)], axis=0)
    rope_tabs = _rope_tables(seq)

    for l in range(depth):
        last = l == depth - 1
        lambda_init = 0.8 - 0.6 * math.exp(-0.3 * l)
        mod = _modulation(c_all, mod_w[l], mod_b[l])
        mod_tab = mod[:batch + 1].reshape(batch + 1, 6, 1, d)

        wl = w_in[l].astype(BF16)
        o1, o2, o3 = conv_w, conv_w + att_w, conv_w + att_w + rw_w
        proj = functools.partial(_project, xs, norm1_g[l], mod_tab, 0, tm=tm, row_of_tile=row_of_tile)
        zc = proj(wl[:, :o1], BF16, tn=_pick_tile(512, o1), name="proj_conv")
        qkv = proj(wl[:, o1:o2], BF16, tn=_pick_tile(512, att_w), name="proj_qkv")
        zr = proj(wl[:, o2:o3], BF16, tn=rw_w, name="proj_rwkv")
        gates = proj(wl[:, o3:], BF16, tn=_pick_tile(512, 3 * d), name="proj_gates")

        y_conv = _conv_branch(zc, conv_dw_w[l], conv_dw_b[l], conv_ln_g[l], conv_ln_b[l],
                              tc, n_lat, seq, ctx_len)

        lam_vecs = jnp.stack([att_lq1[l], att_lk1[l], att_lq2[l], att_lk2[l]])
        y_att = _attention(qkv, rope_tabs, lam_vecs, att_subln_g[l], lambda_init, batch, seq, ctx_len)
        if not last:
            y_att = _attention(qkv, None, lam_vecs, att_subln_g[l], lambda_init, batch, seq, ctx_len, y_att)

        lp = dict(rwkv_shift=rwkv_shift[l], rwkv_w0=rwkv_w0[l], rwkv_w2=rwkv_w2[l], rwkv_a0=rwkv_a0[l],
                  rwkv_a2=rwkv_a2[l], rwkv_g2=rwkv_g2[l], rwkv_kk=rwkv_kk[l], rwkv_ka=rwkv_ka[l],
                  rwkv_rk=rwkv_rk[l])
        r, v, kk, g, bonus, lw, kd, bvec = _rwkv_prep(zr, lp, tc, n_lat, seq, ctx_len)
        yf, yb = _rwkv_scan(r, lw, kd, v, kk, bvec, batch, seq, ctx_len)
        y_rw = _rwkv_out(yf, yb, bonus, g, rwkv_gn_g[l], rwkv_gn_b[l], tc)

        rows = n_lat if last else xs.shape[0]
        xs = _merge(xs, rows, y_conv, y_att, y_rw, gates, p_conv[l].astype(BF16), p_att[l].astype(BF16),
                    p_rwkv[l].astype(BF16), w_out[l].astype(BF16), mod_tab, tmm, row_of_tile_m)
        xs = _mlp(xs, rows, norm2_g[l], mod_tab, mlp_w1[l].astype(BF16), mlp_w2[l].astype(BF16),
                  final_g if last else None, tm, _pick_tile(512, mlp_w1.shape[2]), row_of_tile)
    return xs.reshape(batch, seq, d)
```

```python
import functools
import math

import jax
import jax.numpy as jnp
from jax import lax
from jax.experimental import pallas as pl
from jax.experimental.pallas import tpu as pltpu

F32 = jnp.float32
BF16 = jnp.bfloat16

GRID_W = 64
NORM_EPS = 1e-6
CONV_WIDTH = 31
LN_EPS = 1e-5
ATT_HEADS = 4
ATT_HEAD_DIM = 64
ATT_VAL_DIM = 2 * ATT_HEAD_DIM
ROPE_BASE = 10000.0
SUBLN_EPS = 1e-5
RWKV_HEADS = 8
RWKV_HEAD_DIM = 64
RWKV_DIM = RWKV_HEADS * RWKV_HEAD_DIM
DECAY_LORA = 64
ICLR_LORA = 64
GATE_LORA = 128
GN_EPS = 64e-5

LANES = 128
HALO = 16
SCAN_CHUNK = 64
SCAN_SUBCHUNKS = 4
ATT_Q_TILE = 512
ATT_ROW_PARTS = 4
VMEM_LIMIT = 56 << 20


def _cparams(*sem):
    return pltpu.CompilerParams(dimension_semantics=sem, vmem_limit_bytes=VMEM_LIMIT)


def _pick_tile(limit, *sizes):
    t = limit
    while any(s % t for s in sizes):
        t //= 2
    return t


def _dot(a, b):
    return jnp.dot(a, b, preferred_element_type=F32)


def _dot_nt(a, b):
    return lax.dot_general(a, b, (((1,), (1,)), ((), ())), preferred_element_type=F32)


def _dot_tn(a, b):
    return lax.dot_general(a, b, (((0,), (0,)), ((), ())), preferred_element_type=F32)


def _seg_sum(x, seg):
    hi = x.astype(BF16)
    lo = (x - hi.astype(F32)).astype(BF16)
    return _dot(hi, seg) + _dot(lo, seg)


def _modulated_norm(x, gain, shift, scale):
    y = x * lax.rsqrt(jnp.mean(x * x, axis=-1, keepdims=True) + NORM_EPS)
    return (y * gain) * (1.0 + scale) + shift


def _mod_kernel(c_ref, w_ref, b_ref, o_ref):
    c = c_ref[...]
    s = c * jax.nn.sigmoid(c)
    o_ref[...] = _dot(s.astype(BF16), w_ref[...].astype(BF16)) + b_ref[...]


def _modulation(c_all, mod_w, mod_b):
    rows, d = c_all.shape
    n = mod_w.shape[1]
    tn = _pick_tile(1024, n // 6)
    return pl.pallas_call(
        _mod_kernel,
        grid=(n // tn,),
        in_specs=[pl.BlockSpec((rows, d), lambda j: (0, 0)),
                  pl.BlockSpec((d, tn), lambda j: (0, j)),
                  pl.BlockSpec((1, tn), lambda j: (0, j))],
        out_specs=pl.BlockSpec((rows, tn), lambda j: (0, j)),
        out_shape=jax.ShapeDtypeStruct((rows, n), F32),
        compiler_params=_cparams("arbitrary"),
        name="adaln_modulation",
    )(c_all, mod_w, mod_b.reshape(1, n))


def _proj_kernel(x_ref, g_ref, sh_ref, sc_ref, w_ref, o_ref, h_ref):
    @pl.when(pl.program_id(1) == 0)
    def _():
        h_ref[...] = _modulated_norm(x_ref[...], g_ref[...], sh_ref[...], sc_ref[...]).astype(BF16)

    o_ref[...] = _dot(h_ref[...], w_ref[...]).astype(o_ref.dtype)


def _project(x, gain, mod_tab, which_shift, w, out_dtype, tm, tn, row_of_tile, name):
    m, d = x.shape
    n = w.shape[1]

    def tab_spec(which):
        return pl.BlockSpec((None, None, 1, d), lambda i, j: (row_of_tile(i), which, 0, 0))

    return pl.pallas_call(
        _proj_kernel,
        grid=(m // tm, n // tn),
        in_specs=[pl.BlockSpec((tm, d), lambda i, j: (i, 0)),
                  pl.BlockSpec((1, d), lambda i, j: (0, 0)),
                  tab_spec(which_shift), tab_spec(which_shift + 1),
                  pl.BlockSpec((d, tn), lambda i, j: (0, j))],
        out_specs=pl.BlockSpec((tm, tn), lambda i, j: (i, j)),
        out_shape=jax.ShapeDtypeStruct((m, n), out_dtype),
        scratch_shapes=[pltpu.VMEM((tm, d), BF16)],
        compiler_params=_cparams("parallel", "arbitrary"),
        name=name,
    )(x, gain.reshape(1, d), mod_tab, mod_tab, w)


def _seq_edges(i, tile, n_lat_rows, seq, ctx_len):
    n_lat_tiles = n_lat_rows // tile
    is_lat = i < n_lat_tiles
    per = jnp.where(is_lat, seq // tile, ctx_len // tile)
    pos = jnp.where(is_lat, i, i - n_lat_tiles) % per
    return pos == 0, pos == per - 1


def _conv_kernel(z_ref, zp_ref, zn_ref, w_ref, b_ref, lg_ref, lb_ref, o_ref, ext_ref,
                 *, tile, n_lat_rows, seq, ctx_len):
    first, last = _seq_edges(pl.program_id(0), tile, n_lat_rows, seq, ctx_len)
    cd = o_ref.shape[1]

    def glu(z):
        z = z.astype(F32)
        return z[:, :cd] * jax.nn.sigmoid(z[:, cd:])

    ext_ref[0:HALO, :] = jnp.where(first, 0.0, glu(zp_ref[...]))
    ext_ref[HALO:HALO + tile, :] = glu(z_ref[...])
    ext_ref[HALO + tile:, :] = jnp.where(last, 0.0, glu(zn_ref[...]))
    half = CONV_WIDTH // 2
    acc = jnp.zeros((tile, cd), F32)
    for k in range(CONV_WIDTH):
        acc = acc + w_ref[k:k + 1, :] * ext_ref[pl.ds(HALO - half + k, tile), :]
    u = acc + b_ref[...]
    mu = jnp.mean(u, axis=-1, keepdims=True)
    dlt = u - mu
    var = jnp.mean(dlt * dlt, axis=-1, keepdims=True)
    y = dlt * lax.rsqrt(var + LN_EPS) * lg_ref[...] + lb_ref[...]
    o_ref[...] = (y * jax.nn.sigmoid(y)).astype(o_ref.dtype)


def _halo_specs(tile, width, m):
    per = tile // HALO
    last_blk = m // HALO - 1
    return [pl.BlockSpec((tile, width), lambda i: (i, 0)),
            pl.BlockSpec((HALO, width), lambda i: (jnp.maximum(i * per - 1, 0), 0)),
            pl.BlockSpec((HALO, width), lambda i: (jnp.minimum((i + 1) * per, last_blk), 0))]


def _conv_branch(zc, dw_w, dw_b, ln_g, ln_b, tile, n_lat_rows, seq, ctx_len):
    m, width = zc.shape
    cd = width // 2
    vec = pl.BlockSpec((1, cd), lambda i: (0, 0))
    return pl.pallas_call(
        functools.partial(_conv_kernel, tile=tile, n_lat_rows=n_lat_rows, seq=seq, ctx_len=ctx_len),
        grid=(m // tile,),
        in_specs=_halo_specs(tile, width, m) + [pl.BlockSpec((CONV_WIDTH, cd), lambda i: (0, 0)),
                                                vec, vec, vec],
        out_specs=pl.BlockSpec((tile, cd), lambda i: (i, 0)),
        out_shape=jax.ShapeDtypeStruct((m, cd), BF16),
        scratch_shapes=[pltpu.VMEM((tile + 2 * HALO, cd), F32)],
        compiler_params=_cparams("parallel"),
        name="conv_branch",
    )(zc, zc, zc, dw_w, dw_b.reshape(1, cd), ln_g.reshape(1, cd), ln_b.reshape(1, cd))


def _rope(x, cos, sin_next, sin_prev):
    return x * cos + pltpu.roll(x, LANES - 1, 1) * sin_next + pltpu.roll(x, 1, 1) * sin_prev


def _attn_kernel(*refs, lambda_init, n_ctx, with_lat):
    if with_lat:
        (q_ref, kc_ref, vc_ref, kl_ref, vl_ref, cq_ref, snq_ref, spq_ref, ck_ref, snk_ref, spk_ref,
         lam_ref, g_ref, o_ref, ks_ref, vs_ref) = refs
    else:
        q_ref, kc_ref, vc_ref, lam_ref, g_ref, _, o_ref, ks_ref, vs_ref = refs
    dv = ATT_VAL_DIM

    @pl.when(pl.program_id(2) == 0)
    def _():
        vs_ref[:, dv:] = jnp.ones((vs_ref.shape[0], dv), BF16)
        ks_ref[0:n_ctx, :] = kc_ref[...]
        vs_ref[0:n_ctx, 0:dv] = vc_ref[...]
        if with_lat:
            k = _rope(kl_ref[...].astype(F32), ck_ref[...], snk_ref[...], spk_ref[...])
            ks_ref[n_ctx:, :] = k.astype(BF16)
            vs_ref[n_ctx:, 0:dv] = vl_ref[...]

    tq = q_ref.shape[0]
    q = q_ref[...].astype(F32)
    if with_lat:
        q = _rope(q, cq_ref[...], snq_ref[...], spq_ref[...])
    q = q * (math.log2(math.e) / math.sqrt(ATT_HEAD_DIM))
    first_map = lax.broadcasted_iota(jnp.int32, q.shape, 1) < ATT_HEAD_DIM
    qs = jnp.concatenate([jnp.where(first_map, q, 0.0), jnp.where(first_map, 0.0, q)], axis=0).astype(BF16)
    part = (2 * tq) // ATT_ROW_PARTS
    scores = lambda i: _dot_nt(qs[i * part:(i + 1) * part], ks_ref[...])
    s_next = scores(0)
    outs = []
    for i in range(ATT_ROW_PARTS):
        s = s_next
        if i + 1 < ATT_ROW_PARTS:
            s_next = scores(i + 1)
        p = jnp.exp2((s - jnp.max(s, axis=-1, keepdims=True)).astype(BF16))
        outs.append(_dot(p, vs_ref[...]))
    ol = jnp.concatenate(outs, axis=0)
    o = ol[:, 0:dv] / ol[:, dv:]
    lq = lam_ref[...]
    lam = (jnp.exp(jnp.sum(lq[0:1] * lq[1:2], axis=-1, keepdims=True))
           - jnp.exp(jnp.sum(lq[2:3] * lq[3:4], axis=-1, keepdims=True)) + lambda_init)
    a = o[:tq] - lam * o[tq:]
    y = a * lax.rsqrt(jnp.mean(a * a, axis=-1, keepdims=True) + SUBLN_EPS) * g_ref[...]
    o_ref[...] = (y * (1.0 - lambda_init)).astype(o_ref.dtype)


def _attention(qkv, rope_tabs, lam_vecs, subln_g, lambda_init, batch, seq, ctx_len, y_lat=None):
    m = qkv.shape[0]
    latent_queries = y_lat is None
    h_ = ATT_HEADS
    ctx_blk0 = batch * seq // ctx_len
    tq = _pick_tile(ATT_Q_TILE, seq if latent_queries else ctx_len)
    nq = (seq if latent_queries else ctx_len) // tq
    q0 = 0 if latent_queries else batch * seq // tq
    blk = lambda rows, fn: pl.BlockSpec((rows, LANES), fn)
    in_specs = [blk(tq, lambda b, h, i: (q0 + b * nq + i, h)),
                blk(ctx_len, lambda b, h, i: (ctx_blk0 + b, h_ + h)),
                blk(ctx_len, lambda b, h, i: (ctx_blk0 + b, 2 * h_ + h))]
    args = [qkv, qkv, qkv]
    n_keys = ctx_len
    if latent_queries:
        n_keys += seq
        in_specs += [blk(seq, lambda b, h, i: (b, h_ + h)), blk(seq, lambda b, h, i: (b, 2 * h_ + h))]
        in_specs += [blk(tq, lambda b, h, i: (i, 0))] * 3 + [blk(seq, lambda b, h, i: (0, 0))] * 3
        args += [qkv, qkv] + list(rope_tabs) + list(rope_tabs)
    in_specs += [pl.BlockSpec((4, ATT_HEAD_DIM), lambda b, h, i: (0, 0)),
                 pl.BlockSpec((1, ATT_VAL_DIM), lambda b, h, i: (0, 0))]
    args += [lam_vecs, subln_g.reshape(1, ATT_VAL_DIM)]
    aliases = {}
    if not latent_queries:
        in_specs.append(pl.BlockSpec(memory_space=pl.ANY))
        args.append(y_lat)
        aliases = {len(args) - 1: 0}
    return pl.pallas_call(
        functools.partial(_attn_kernel, lambda_init=lambda_init, n_ctx=ctx_len, with_lat=latent_queries),
        grid=(batch, h_, nq),
        in_specs=in_specs,
        out_specs=blk(tq, lambda b, h, i: (q0 + b * nq + i, h)),
        out_shape=jax.ShapeDtypeStruct((m, h_ * ATT_VAL_DIM), BF16),
        scratch_shapes=[pltpu.VMEM((n_keys, LANES), BF16), pltpu.VMEM((n_keys, 2 * ATT_VAL_DIM), BF16)],
        input_output_aliases=aliases,
        compiler_params=_cparams("parallel", "parallel", "arbitrary"),
        name="diff_attention_latent" if latent_queries else "diff_attention_context",
    )(*args)


def _rope_tables(seq):
    t = jnp.arange(seq, dtype=jnp.int32)
    row = (t // GRID_W).astype(F32)
    col = (t % GRID_W).astype(F32)
    n_pairs = ATT_HEAD_DIM // 4
    inv_freq = ROPE_BASE ** (-jnp.arange(n_pairs, dtype=F32) / n_pairs)
    ang = jnp.concatenate([row[:, None] * inv_freq, col[:, None] * inv_freq], axis=-1)
    cos = jnp.repeat(jnp.cos(ang), 2, axis=-1)
    sin = jnp.repeat(jnp.sin(ang), 2, axis=-1)
    even = (jnp.arange(ATT_HEAD_DIM) % 2 == 0)[None, :]
    both = lambda a: jnp.concatenate([a, a], axis=-1)
    return both(cos), both(jnp.where(even, -sin, 0.0)), both(jnp.where(even, 0.0, sin))


def _rwkv_prep_kernel(z_ref, zp_ref, zn_ref, sh_ref, w2_ref, a2_ref, g2_ref, w0_ref, a0_ref,
                      kkw_ref, ka_ref, rk_ref, seg_ref,
                      r_o, v_o, kk_o, g_o, bonus_o, lw_o, kd_o, b_o, ext_ref,
                      *, tile, n_lat_rows, seq, ctx_len):
    first, last = _seq_edges(pl.program_id(0), tile, n_lat_rows, seq, ctx_len)
    hd = RWKV_DIM
    ext_ref[0:HALO, :] = jnp.where(first, 0.0, zp_ref[...].astype(F32))
    ext_ref[HALO:HALO + tile, :] = z_ref[...].astype(F32)
    ext_ref[HALO + tile:, :] = jnp.where(last, 0.0, zn_ref[...].astype(F32))
    z = (sh_ref[0:1, :] * ext_ref[pl.ds(HALO - 1, tile), :]
         + sh_ref[1:2, :] * ext_ref[pl.ds(HALO, tile), :]
         + sh_ref[2:3, :] * ext_ref[pl.ds(HALO + 1, tile), :])
    seg = seg_ref[...]
    r = z[:, 0:hd]
    k = z[:, hd:2 * hd]
    v = z[:, 2 * hd:3 * hd]
    o = 3 * hd
    wd = z[:, o:o + 2 * DECAY_LORA]
    ad = z[:, o + 2 * DECAY_LORA:o + 2 * DECAY_LORA + 2 * ICLR_LORA]
    gd = z[:, o + 2 * DECAY_LORA + 2 * ICLR_LORA:]
    kkf = k * kkw_ref[...]
    kk = kkf * lax.rsqrt(jnp.maximum(_seg_sum(kkf * kkf, seg), 1e-24))
    g = _dot(jax.nn.sigmoid(gd).astype(BF16), g2_ref[...])
    wl = w0_ref[...] + _dot(jnp.tanh(wd).astype(BF16), w2_ref[...])
    al = a0_ref[...] + _dot(ad.astype(BF16), a2_ref[...])
    softplus = jnp.maximum(-wl, 0.0) + jnp.log(1.0 + jnp.exp(-jnp.abs(wl)))
    lw = -jnp.exp(-softplus - 0.5)
    a = jax.nn.sigmoid(al)
    r_o[...] = r
    v_o[...] = v
    kk_o[...] = kk
    g_o[...] = g
    lw_o[...] = lw
    rk = rk_ref[...]
    bonus = jnp.zeros_like(r)
    for d in range(2):
        a_d = a[:, d * hd:(d + 1) * hd]
        kd = k * (1.0 + (a_d - 1.0) * ka_ref[...])
        kd_o[:, d * hd:(d + 1) * hd] = kd
        b_o[:, d * hd:(d + 1) * hd] = kk * a_d
        bonus = bonus + _seg_sum(r * kd * rk, seg) * v
    bonus_o[...] = bonus


def _block_diag2(w):
    z = jnp.zeros_like(w[0])
    return jnp.concatenate([jnp.concatenate([w[0], z], axis=1), jnp.concatenate([z, w[1]], axis=1)], axis=0)


def _head_seg_matrix(n, head):
    idx = jnp.arange(n) // head
    return (idx[:, None] == idx[None, :]).astype(BF16)


def _rwkv_prep(zr, lp, tile, n_lat_rows, seq, ctx_len):
    m, width = zr.shape
    hd = RWKV_DIM
    full = lambda a: pl.BlockSpec(a.shape, lambda i: (0,) * a.ndim)
    consts = [lp['rwkv_shift'],
              _block_diag2(lp['rwkv_w2']).astype(BF16), _block_diag2(lp['rwkv_a2']).astype(BF16),
              lp['rwkv_g2'].astype(BF16),
              lp['rwkv_w0'].reshape(1, 2 * hd), lp['rwkv_a0'].reshape(1, 2 * hd),
              lp['rwkv_kk'].reshape(1, hd), lp['rwkv_ka'].reshape(1, hd), lp['rwkv_rk'].reshape(1, hd),
              _head_seg_matrix(hd, RWKV_HEAD_DIM)]
    one = jax.ShapeDtypeStruct((m, hd), F32)
    two = jax.ShapeDtypeStruct((m, 2 * hd), F32)
    ospec = lambda n: pl.BlockSpec((tile, n), lambda i: (i, 0))
    return pl.pallas_call(
        functools.partial(_rwkv_prep_kernel, tile=tile, n_lat_rows=n_lat_rows, seq=seq, ctx_len=ctx_len),
        grid=(m // tile,),
        in_specs=_halo_specs(tile, width, m) + [full(a) for a in consts],
        out_specs=[ospec(hd)] * 5 + [ospec(2 * hd)] * 3,
        out_shape=[one] * 5 + [two] * 3,
        scratch_shapes=[pltpu.VMEM((tile + 2 * HALO, width), F32)],
        compiler_params=_cparams("parallel"),
        name="rwkv_prep",
    )(zr, zr, zr, *consts)


def _split3(x):
    h1 = x.astype(BF16)
    r1 = x - h1.astype(F32)
    h2 = r1.astype(BF16)
    h3 = (r1 - h2.astype(F32)).astype(BF16)
    return h1, h2, h3


def _scan_kernel(*refs):
    ins = (refs[0:6], refs[6:12])
    y_refs = refs[12:14]
    s_ref = refs[14]
    c_len = SCAN_CHUNK
    n_sub = refs[0].shape[0] // c_len
    n_pairs = RWKV_DIM // LANES

    @pl.when(pl.program_id(1) == 0)
    def _():
        s_ref[...] = jnp.zeros_like(s_ref)

    row = lax.broadcasted_iota(jnp.int32, (c_len, c_len), 0)
    col = lax.broadcasted_iota(jnp.int32, (c_len, c_len), 1)
    row2 = lax.broadcasted_iota(jnp.int32, (c_len, 2 * c_len), 0)
    col2 = lax.broadcasted_iota(jnp.int32, (c_len, 2 * c_len), 1)
    second_half = col2 >= c_len
    col2 = jnp.where(second_half, col2 - c_len, col2)
    eye = jnp.where(row == col, 1.0, 0.0)
    lane = lax.broadcasted_iota(jnp.int32, (c_len, LANES), 1)
    head0 = lane < RWKV_HEAD_DIM
    srow = lax.broadcasted_iota(jnp.int32, (LANES, LANES), 0) // RWKV_HEAD_DIM
    scol = lax.broadcasted_iota(jnp.int32, (LANES, LANES), 1) // RWKV_HEAD_DIM
    same_head = srow == scol

    masks, ops = [], {}
    for d in range(2):
        ahead = (row - col) if d == 0 else (col - row)
        ahead2 = (row2 - col2) if d == 0 else (col2 - row2)
        strict = ahead > 0
        tri = jnp.where(ahead >= 0, 1.0, 0.0).astype(BF16)
        pair_masks = []
        for sh in range(int(math.log2(c_len))):
            same_pair = (row >> (sh + 1)) == (col >> (sh + 1))
            pair_masks.append(same_pair & ((row >> sh) != (col >> sh)) & strict)
        masks.append(dict(pair=pair_masks,
                          ak=second_half & (ahead2 > 0),
                          y=ahead2 >= 0))
        r_ref, lw_ref, k_ref, v_ref, kk_ref, b_ref = ins[d]
        for g in range(n_sub):
            rows = pl.ds(g * c_len, c_len)
            lw = lw_ref[rows, :]
            h1, h2, h3 = _split3(lw)
            c = _dot(tri, h1) + _dot(tri, h2) + _dot(tri, h3)
            tot = jnp.sum(lw, axis=0, keepdims=True)
            e_neg = jnp.exp(-c)
            dec = jnp.exp(tot - c)
            ops[d, g] = dict(
                a_t=-kk_ref[rows, :] * jnp.exp(c - lw), r_t=r_ref[rows, :] * jnp.exp(c),
                b_t=(b_ref[rows, :] * e_neg).astype(BF16), k_t=(k_ref[rows, :] * e_neg).astype(BF16),
                b_p=(b_ref[rows, :] * dec).astype(BF16), k_p=(k_ref[rows, :] * dec).astype(BF16),
                p_tot=jnp.exp(tot), v=v_ref[rows, :].astype(BF16))

    groups = [(d, g, p) for d in range(2) for g in range(n_sub) for p in range(n_pairs)]
    chains = [gr + (hh,) for gr in groups for hh in range(2)]
    lanes_of = lambda p: slice(p * LANES, (p + 1) * LANES)

    scores, keep_of = {}, {}
    for ch in chains:
        d, g, p, hh = ch
        sl = lanes_of(p)
        keep_of[ch] = head0 if hh == 0 else jnp.logical_not(head0)
        lhs = jnp.concatenate([jnp.where(keep_of[ch], ops[d, g]['a_t'][:, sl], 0.0),
                               jnp.where(keep_of[ch], ops[d, g]['r_t'][:, sl], 0.0)], axis=0).astype(BF16)
        rhs = jnp.concatenate([ops[d, g]['b_t'][:, sl], ops[d, g]['k_t'][:, sl]], axis=0)
        scores[ch] = _dot_nt(lhs, rhs)
    vv = {}
    for gr in groups:
        d, g, p = gr
        v_p = ops[d, g]['v'][:, lanes_of(p)]
        vv[gr] = jnp.concatenate([v_p, v_p], axis=0)
    w0, l_mat, x_inv = {}, {}, {}
    for ch in chains:
        d, g, p, hh = ch
        top = scores[ch][:c_len]
        w0[ch] = _dot(jnp.where(masks[d]['ak'], top, 0.0).astype(BF16), vv[d, g, p])
        l_mat[ch] = top[:, :c_len]
        x_inv[ch] = eye + jnp.where(masks[d]['pair'][0], l_mat[ch], 0.0)

    for level in range(1, int(math.log2(c_len))):
        t1 = {}
        for ch in chains:
            pm = masks[ch[0]]['pair'][level]
            t1[ch] = _dot(jnp.where(pm, l_mat[ch], 0.0).astype(BF16), x_inv[ch].astype(BF16))
        for ch in chains:
            x_inv[ch] = x_inv[ch] + _dot(x_inv[ch].astype(BF16), t1[ch].astype(BF16))
    xa_h, u0_h = {}, {}
    for ch in chains:
        d, g, p, hh = ch
        xb = x_inv[ch].astype(BF16)
        xa_h[ch] = _dot(xb, jnp.where(keep_of[ch], ops[d, g]['a_t'][:, lanes_of(p)], 0.0).astype(BF16))
        u0_h[ch] = _dot(xb, w0[ch].astype(BF16))
    p_rows = [ops[d, g]['p_tot'] for d in range(2) for g in range(n_sub)]
    p_all = jnp.concatenate(p_rows + [jnp.zeros((LANES - len(p_rows), RWKV_DIM), F32)], axis=0)
    p_t = [jnp.transpose(p_all[:, lanes_of(p)]) for p in range(n_pairs)]
    bk_t = {}
    for d, g, p in groups:
        sl = lanes_of(p)
        bk_t[d, g, p] = jnp.transpose(jnp.concatenate([ops[d, g]['b_p'][:, sl], ops[d, g]['k_p'][:, sl]], axis=0))
    lhs_t, u0_pair, p_col = {}, {}, {}
    for gr in groups:
        d, g, p = gr
        xa = (xa_h[gr + (0,)] + xa_h[gr + (1,)]).astype(BF16)
        u0_pair[gr] = jnp.where(head0, u0_h[gr + (0,)], u0_h[gr + (1,)])
        lhs_t[gr] = jnp.concatenate([xa, ops[d, g]['r_t'][:, lanes_of(p)].astype(BF16)], axis=0)
        r_idx = d * n_sub + g
        p_col[gr] = jnp.broadcast_to(p_t[p][:, r_idx:r_idx + 1], (LANES, LANES))

    state = {(d, p): s_ref[d, p] for d in range(2) for p in range(n_pairs)}
    for step in range(n_sub):
        sub_of = lambda d: step if d == 0 else n_sub - 1 - step
        live = [(d, sub_of(d), p) for d in range(2) for p in range(n_pairs)]
        prod = {gr: _dot(lhs_t[gr], state[gr[0], gr[2]].astype(BF16)) for gr in live}
        uv = {gr: jnp.concatenate([(prod[gr][:c_len] + u0_pair[gr]).astype(BF16), vv[gr][:c_len]], axis=0)
              for gr in live}
        upd = {gr: _dot(bk_t[gr], uv[gr]) for gr in live}
        for gr in live:
            d, g, p = gr
            state[d, p] = p_col[gr] * state[d, p] + jnp.where(same_head, upd[gr], 0.0)
        for gr in live:
            d, g, p = gr
            ys = [_dot(jnp.where(masks[d]['y'], scores[gr + (hh,)][c_len:], 0.0).astype(BF16), uv[gr])
                  for hh in range(2)]
            y_refs[d][pl.ds(g * c_len, c_len), lanes_of(p)] = prod[gr][c_len:] + jnp.where(head0, ys[0], ys[1])
    for (d, p), s_new in state.items():
        s_ref[d, p] = s_new


def _rwkv_scan(r, lw, kd, v, kk, bvec, batch, seq, ctx_len):
    m, hd = r.shape
    n_sub = _pick_tile(SCAN_SUBCHUNKS, ctx_len // SCAN_CHUNK, seq // SCAN_CHUNK)
    c_len = n_sub * SCAN_CHUNK
    n_ctx, n_lat = ctx_len // c_len, seq // c_len
    ctx0 = batch * seq // c_len

    def rowblk(d):
        def fn(b, j):
            jc = j if d == 0 else n_ctx - 1 - j
            jl = (j - n_ctx) if d == 0 else n_lat - 1 - (j - n_ctx)
            return jnp.where(j < n_ctx, ctx0 + b * n_ctx + jc, b * n_lat + jl)
        return fn

    in_specs, args = [], []
    for d in range(2):
        shared = pl.BlockSpec((c_len, hd), lambda b, j, f=rowblk(d): (f(b, j), 0))
        per_dir = pl.BlockSpec((c_len, hd), lambda b, j, f=rowblk(d), d=d: (f(b, j), d))
        in_specs += [shared, per_dir, per_dir, shared, shared, per_dir]
        args += [r, lw, kd, v, kk, bvec]
    out = jax.ShapeDtypeStruct((m, hd), F32)
    return pl.pallas_call(
        _scan_kernel,
        grid=(batch, n_ctx + n_lat),
        in_specs=in_specs,
        out_specs=[pl.BlockSpec((c_len, hd), lambda b, j, f=rowblk(d): (f(b, j), 0)) for d in range(2)],
        out_shape=[out, out],
        scratch_shapes=[pltpu.VMEM((2, hd // LANES, LANES, LANES), F32)],
        compiler_params=_cparams("parallel", "arbitrary"),
        name="rwkv_scan",
    )(*args)


def _rwkv_out_kernel(yf_ref, yb_ref, bonus_ref, g_ref, gg_ref, gb_ref, seg_ref, o_ref):
    seg = seg_ref[...]
    ys = yf_ref[...] + yb_ref[...]
    inv_n = 1.0 / RWKV_HEAD_DIM
    mu = _seg_sum(ys, seg) * inv_n
    dlt = ys - mu
    var = _seg_sum(dlt * dlt, seg) * inv_n
    yn = dlt * lax.rsqrt(var + GN_EPS) * gg_ref[...] + gb_ref[...]
    o_ref[...] = ((yn + bonus_ref[...]) * g_ref[...]).astype(o_ref.dtype)


def _rwkv_out(yf, yb, bonus, g, gn_g, gn_b, tile):
    m, hd = yf.shape
    seg = _head_seg_matrix(hd, RWKV_HEAD_DIM)
    tok = pl.BlockSpec((tile, hd), lambda i: (i, 0))
    vec = pl.BlockSpec((1, hd), lambda i: (0, 0))
    return pl.pallas_call(
        _rwkv_out_kernel,
        grid=(m // tile,),
        in_specs=[tok, tok, tok, tok, vec, vec, pl.BlockSpec((hd, hd), lambda i: (0, 0))],
        out_specs=tok,
        out_shape=jax.ShapeDtypeStruct((m, hd), BF16),
        compiler_params=_cparams("parallel"),
        name="rwkv_out",
    )(yf, yb, bonus, g, gn_g.reshape(1, hd), gn_b.reshape(1, hd), seg)


def _merge_kernel(x_ref, yc_ref, ya_ref, yr_ref, gt_ref, pc_ref, pa_ref, pr_ref, wo_ref, gate_ref, o_ref):
    d = x_ref.shape[1]
    g = jax.nn.sigmoid(gt_ref[...].astype(F32))
    mix = (g[:, 0:d] * _dot(yc_ref[...], pc_ref[...])
           + g[:, d:2 * d] * _dot(ya_ref[...], pa_ref[...])
           + g[:, 2 * d:] * _dot(yr_ref[...], pr_ref[...]))
    o_ref[...] = x_ref[...] + gate_ref[...] * _dot(mix.astype(BF16), wo_ref[...])


def _merge(x, rows, yc, ya, yr, gates, pc, pa, pr, wo, mod_tab, tm, row_of_tile):
    d = x.shape[1]
    tok = lambda a: pl.BlockSpec((tm, a.shape[1]), lambda i: (i, 0))
    full = lambda a: pl.BlockSpec(a.shape, lambda i: (0, 0))
    return pl.pallas_call(
        _merge_kernel,
        grid=(rows // tm,),
        in_specs=[tok(x), tok(yc), tok(ya), tok(yr), tok(gates), full(pc), full(pa), full(pr), full(wo),
                  pl.BlockSpec((None, None, 1, d), lambda i: (row_of_tile(i), 2, 0, 0))],
        out_specs=tok(x),
        out_shape=jax.ShapeDtypeStruct((rows, d), F32),
        compiler_params=_cparams("parallel"),
        name="merge_out_proj",
    )(x, yc, ya, yr, gates, pc, pa, pr, wo, mod_tab)


def _mlp_kernel(*refs, final):
    if final:
        x_ref, g_ref, sh_ref, sc_ref, gate_ref, w1_ref, w2_ref, fg_ref, o_ref, h_ref, acc_ref = refs
    else:
        x_ref, g_ref, sh_ref, sc_ref, gate_ref, w1_ref, w2_ref, o_ref, h_ref, acc_ref = refs
    f = pl.program_id(1)

    @pl.when(f == 0)
    def _():
        h_ref[...] = _modulated_norm(x_ref[...], g_ref[...], sh_ref[...], sc_ref[...]).astype(BF16)
        acc_ref[...] = jnp.zeros_like(acc_ref)

    t = jnp.maximum(_dot(h_ref[...], w1_ref[...]), 0.0)
    acc_ref[...] += _dot((t * t).astype(BF16), w2_ref[...])

    @pl.when(f == pl.num_programs(1) - 1)
    def _():
        y = x_ref[...] + gate_ref[...] * acc_ref[...]
        if final:
            y = y * lax.rsqrt(jnp.mean(y * y, axis=-1, keepdims=True) + NORM_EPS) * fg_ref[...]
        o_ref[...] = y


def _mlp(x, rows, gain, mod_tab, w1, w2, final_g, tm, tf, row_of_tile):
    d = x.shape[1]
    dff = w1.shape[1]
    final = final_g is not None
    tab = lambda which: pl.BlockSpec((None, None, 1, d), lambda i, f: (row_of_tile(i), which, 0, 0))
    in_specs = [pl.BlockSpec((tm, d), lambda i, f: (i, 0)),
                pl.BlockSpec((1, d), lambda i, f: (0, 0)),
                tab(3), tab(4), tab(5),
                pl.BlockSpec((d, tf), lambda i, f: (0, f)),
                pl.BlockSpec((tf, d), lambda i, f: (f, 0))]
    args = [x, gain.reshape(1, d), mod_tab, mod_tab, mod_tab, w1, w2]
    if final:
        in_specs.append(pl.BlockSpec((1, d), lambda i, f: (0, 0)))
        args.append(final_g.reshape(1, d))
    return pl.pallas_call(
        functools.partial(_mlp_kernel, final=final),
        grid=(rows // tm, dff // tf),
        in_specs=in_specs,
        out_specs=pl.BlockSpec((tm, d), lambda i, f: (i, 0)),
        out_shape=jax.ShapeDtypeStruct((rows, d), F32),
        scratch_shapes=[pltpu.VMEM((tm, d), BF16), pltpu.VMEM((tm, d), F32)],
        compiler_params=_cparams("parallel", "arbitrary"),
        name="mlp_final" if final else "mlp",
    )(*args)


def kernel(x, c, ctx, c_ctx, mod_w, mod_b, norm1_g, norm2_g, w_in, conv_dw_w, conv_dw_b, conv_ln_g, conv_ln_b, p_conv, att_lq1, att_lk1, att_lq2, att_lk2, att_subln_g, p_att, rwkv_shift, rwkv_w0, rwkv_w2, rwkv_a0, rwkv_a2, rwkv_g2, rwkv_kk, rwkv_ka, rwkv_rk, rwkv_gn_g, rwkv_gn_b, p_rwkv, w_out, mlp_w1, mlp_w2, final_g):
    batch, seq, d = x.shape
    ctx_len = ctx.shape[1]
    depth = mod_w.shape[0]
    n_lat = batch * seq
    conv_w = 2 * conv_dw_w.shape[2]
    att_w = 3 * ATT_HEADS * ATT_VAL_DIM
    rw_w = 3 * RWKV_DIM + 2 * DECAY_LORA + 2 * ICLR_LORA + GATE_LORA
    assert w_in.shape[2] == conv_w + att_w + rw_w + 3 * d
    assert seq % SCAN_CHUNK == 0 and ctx_len % SCAN_CHUNK == 0 and (batch * seq) % ctx_len == 0

    tm = _pick_tile(1024, seq, batch * ctx_len)
    tmm = _pick_tile(512, seq, batch * ctx_len)
    tc = _pick_tile(256, seq, ctx_len)
    tiles_per_seq = seq // tm
    row_of_tile = lambda i: jnp.minimum(i // tiles_per_seq, batch)
    row_of_tile_m = lambda i: jnp.minimum(i // (seq // tmm), batch)

    xs = jnp.concatenate([x.reshape(n_lat, d), ctx.reshape(batch * ctx_len, d)], axis=0)
    pad = (-(batch + 1)) % 8
    c_all = jnp.concatenate([c, c_ctx[None, :], jnp.zeros((pad, d), F32)], axis=0)
    rope_tabs = _rope_tables(seq)

    for l in range(depth):
        last = l == depth - 1
        lambda_init = 0.8 - 0.6 * math.exp(-0.3 * l)
        mod = _modulation(c_all, mod_w[l], mod_b[l])
        mod_tab = mod[:batch + 1].reshape(batch + 1, 6, 1, d)

        wl = w_in[l].astype(BF16)
        o1, o2, o3 = conv_w, conv_w + att_w, conv_w + att_w + rw_w
        proj = functools.partial(_project, xs, norm1_g[l], mod_tab, 0, tm=tm, row_of_tile=row_of_tile)
        zc = proj(wl[:, :o1], BF16, tn=o1, name="proj_conv")
        qkv = proj(wl[:, o1:o2], BF16, tn=att_w, name="proj_qkv")
        zr = proj(wl[:, o2:o3], BF16, tn=rw_w, name="proj_rwkv")
        gates = proj(wl[:, o3:], BF16, tn=3 * d, name="proj_gates")

        y_conv = _conv_branch(zc, conv_dw_w[l], conv_dw_b[l], conv_ln_g[l], conv_ln_b[l],
                              tc, n_lat, seq, ctx_len)

        lam_vecs = jnp.stack([att_lq1[l], att_lk1[l], att_lq2[l], att_lk2[l]])
        y_att = _attention(qkv, rope_tabs, lam_vecs, att_subln_g[l], lambda_init, batch, seq, ctx_len)
        if not last:
            y_att = _attention(qkv, None, lam_vecs, att_subln_g[l], lambda_init, batch, seq, ctx_len, y_att)

        lp = dict(rwkv_shift=rwkv_shift[l], rwkv_w0=rwkv_w0[l], rwkv_w2=rwkv_w2[l], rwkv_a0=rwkv_a0[l],
                  rwkv_a2=rwkv_a2[l], rwkv_g2=rwkv_g2[l], rwkv_kk=rwkv_kk[l], rwkv_ka=rwkv_ka[l],
                  rwkv_rk=rwkv_rk[l])
        r, v, kk, g, bonus, lw, kd, bvec = _rwkv_prep(zr, lp, tc, n_lat, seq, ctx_len)
        yf, yb = _rwkv_scan(r, lw, kd, v, kk, bvec, batch, seq, ctx_len)
        y_rw = _rwkv_out(yf, yb, bonus, g, rwkv_gn_g[l], rwkv_gn_b[l], tc)

        rows = n_lat if last else xs.shape[0]
        xs = _merge(xs, rows, y_conv, y_att, y_rw, gates, p_conv[l].astype(BF16), p_att[l].astype(BF16),
                    p_rwkv[l].astype(BF16), w_out[l].astype(BF16), mod_tab, tmm, row_of_tile_m)
        xs = _mlp(xs, rows, norm2_g[l], mod_tab, mlp_w1[l].astype(BF16), mlp_w2[l].astype(BF16),
                  final_g if last else None, tm, _pick_tile(512, mlp_w1.shape[2]), row_of_tile)
    return xs.reshape(batch, seq, d)
```

```python
import functools
import math

import jax
import jax.numpy as jnp
from jax import lax
from jax.experimental import pallas as pl
from jax.experimental.pallas import tpu as pltpu

F32 = jnp.float32
BF16 = jnp.bfloat16

GRID_W = 64
NORM_EPS = 1e-6
CONV_WIDTH = 31
LN_EPS = 1e-5
ATT_HEADS = 4
ATT_HEAD_DIM = 64
ATT_VAL_DIM = 2 * ATT_HEAD_DIM
ROPE_BASE = 10000.0
SUBLN_EPS = 1e-5
RWKV_HEADS = 8
RWKV_HEAD_DIM = 64
RWKV_DIM = RWKV_HEADS * RWKV_HEAD_DIM
DECAY_LORA = 64
ICLR_LORA = 64
GATE_LORA = 128
GN_EPS = 64e-5

LANES = 128
SUBLANES = 8
HALO = 16
SCAN_CHUNK = 64
SCAN_SUBCHUNKS = 4
ATT_Q_TILE = 512
ATT_ROW_PARTS = 4
VMEM_LIMIT = 56 << 20


def _cparams(*sem):
    return pltpu.CompilerParams(dimension_semantics=sem, vmem_limit_bytes=VMEM_LIMIT)


def _pick_tile(limit, *sizes):
    t = limit
    while any(s % t for s in sizes):
        t //= 2
    return t


def _dot(a, b):
    return jnp.dot(a, b, preferred_element_type=F32)


def _dot_nt(a, b):
    return lax.dot_general(a, b, (((1,), (1,)), ((), ())), preferred_element_type=F32)


def _dot_tn(a, b):
    return lax.dot_general(a, b, (((0,), (0,)), ((), ())), preferred_element_type=F32)


def _seg_sum(x, seg):
    hi = x.astype(BF16)
    lo = (x - hi.astype(F32)).astype(BF16)
    return _dot(hi, seg) + _dot(lo, seg)


def _modulated_norm(x, gain, shift, scale):
    y = x * lax.rsqrt(jnp.mean(x * x, axis=-1, keepdims=True) + NORM_EPS)
    return (y * gain) * (1.0 + scale) + shift


def _mod_kernel(c_ref, w_ref, b_ref, o_ref):
    c = c_ref[...]
    s = c * jax.nn.sigmoid(c)
    o_ref[...] = _dot(s.astype(BF16), w_ref[...].astype(BF16)) + b_ref[...]


def _modulation(c_all, mod_w, mod_b):
    rows, d = c_all.shape
    n = mod_w.shape[1]
    tn = _pick_tile(1024, n // 6)
    return pl.pallas_call(
        _mod_kernel,
        grid=(n // tn,),
        in_specs=[pl.BlockSpec((rows, d), lambda j: (0, 0)),
                  pl.BlockSpec((d, tn), lambda j: (0, j)),
                  pl.BlockSpec((1, tn), lambda j: (0, j))],
        out_specs=pl.BlockSpec((rows, tn), lambda j: (0, j)),
        out_shape=jax.ShapeDtypeStruct((rows, n), F32),
        compiler_params=_cparams("arbitrary"),
        name="adaln_modulation",
    )(c_all, mod_w, mod_b.reshape(1, n))


def _proj_kernel(x_ref, g_ref, sh_ref, sc_ref, w_ref, o_ref, h_ref):
    @pl.when(pl.program_id(1) == 0)
    def _():
        h_ref[...] = _modulated_norm(x_ref[...], g_ref[...], sh_ref[...], sc_ref[...]).astype(BF16)

    o_ref[...] = _dot(h_ref[...], w_ref[...]).astype(o_ref.dtype)


def _project(x, gain, mod_tab, which_shift, w, out_dtype, tm, tn, row_of_tile, name):
    m, d = x.shape
    n = w.shape[1]

    def tab_spec(which):
        return pl.BlockSpec((None, None, 1, d), lambda i, j: (row_of_tile(i), which, 0, 0))

    return pl.pallas_call(
        _proj_kernel,
        grid=(m // tm, n // tn),
        in_specs=[pl.BlockSpec((tm, d), lambda i, j: (i, 0)),
                  pl.BlockSpec((1, d), lambda i, j: (0, 0)),
                  tab_spec(which_shift), tab_spec(which_shift + 1),
                  pl.BlockSpec((d, tn), lambda i, j: (0, j))],
        out_specs=pl.BlockSpec((tm, tn), lambda i, j: (i, j)),
        out_shape=jax.ShapeDtypeStruct((m, n), out_dtype),
        scratch_shapes=[pltpu.VMEM((tm, d), BF16)],
        compiler_params=_cparams("parallel", "arbitrary"),
        name=name,
    )(x, gain.reshape(1, d), mod_tab, mod_tab, w)


def _seq_edges(i, tile, n_lat_rows, seq, ctx_len):
    n_lat_tiles = n_lat_rows // tile
    is_lat = i < n_lat_tiles
    per = jnp.where(is_lat, seq // tile, ctx_len // tile)
    pos = jnp.where(is_lat, i, i - n_lat_tiles) % per
    return pos == 0, pos == per - 1


def _conv_kernel(z_ref, zp_ref, zn_ref, w_ref, b_ref, lg_ref, lb_ref, o_ref, ext_ref, rot_ref,
                 *, tile, n_lat_rows, seq, ctx_len):
    first, last = _seq_edges(pl.program_id(0), tile, n_lat_rows, seq, ctx_len)
    cd = o_ref.shape[1]

    def glu(z):
        z = z.astype(F32)
        return z[:, :cd] * jax.nn.sigmoid(z[:, cd:])

    ext_ref[0:HALO, :] = jnp.where(first, 0.0, glu(zp_ref[...]))
    ext_ref[HALO:HALO + tile, :] = glu(z_ref[...])
    ext_ref[HALO + tile:, :] = jnp.where(last, 0.0, glu(zn_ref[...]))
    span = rot_ref.shape[1]
    for r in range(SUBLANES):
        rot_ref[r] = ext_ref[pl.ds(r, span), :]
    half = CONV_WIDTH // 2
    acc = jnp.zeros((tile, cd), F32)
    for k in range(CONV_WIDTH):
        start = HALO - half + k
        acc = acc + w_ref[k:k + 1, :] * rot_ref[start % SUBLANES, pl.ds(start - start % SUBLANES, tile), :]
    u = acc + b_ref[...]
    mu = jnp.mean(u, axis=-1, keepdims=True)
    dlt = u - mu
    var = jnp.mean(dlt * dlt, axis=-1, keepdims=True)
    y = dlt * lax.rsqrt(var + LN_EPS) * lg_ref[...] + lb_ref[...]
    o_ref[...] = (y * jax.nn.sigmoid(y)).astype(o_ref.dtype)


def _halo_specs(tile, width, m):
    per = tile // HALO
    last_blk = m // HALO - 1
    return [pl.BlockSpec((tile, width), lambda i: (i, 0)),
            pl.BlockSpec((HALO, width), lambda i: (jnp.maximum(i * per - 1, 0), 0)),
            pl.BlockSpec((HALO, width), lambda i: (jnp.minimum((i + 1) * per, last_blk), 0))]


def _conv_branch(zc, dw_w, dw_b, ln_g, ln_b, tile, n_lat_rows, seq, ctx_len):
    m, width = zc.shape
    cd = width // 2
    vec = pl.BlockSpec((1, cd), lambda i: (0, 0))
    return pl.pallas_call(
        functools.partial(_conv_kernel, tile=tile, n_lat_rows=n_lat_rows, seq=seq, ctx_len=ctx_len),
        grid=(m // tile,),
        in_specs=_halo_specs(tile, width, m) + [pl.BlockSpec((CONV_WIDTH, cd), lambda i: (0, 0)),
                                                vec, vec, vec],
        out_specs=pl.BlockSpec((tile, cd), lambda i: (i, 0)),
        out_shape=jax.ShapeDtypeStruct((m, cd), BF16),
        scratch_shapes=[pltpu.VMEM((tile + 2 * HALO, cd), F32),
                        pltpu.VMEM((SUBLANES, tile + 2 * HALO - SUBLANES, cd), F32)],
        compiler_params=_cparams("parallel"),
        name="conv_branch",
    )(zc, zc, zc, dw_w, dw_b.reshape(1, cd), ln_g.reshape(1, cd), ln_b.reshape(1, cd))


def _rope(x, cos, sin_next, sin_prev):
    return x * cos + pltpu.roll(x, LANES - 1, 1) * sin_next + pltpu.roll(x, 1, 1) * sin_prev


def _attn_kernel(*refs, lambda_init, n_ctx, with_lat):
    if with_lat:
        (q_ref, kc_ref, vc_ref, kl_ref, vl_ref, cq_ref, snq_ref, spq_ref, ck_ref, snk_ref, spk_ref,
         lam_ref, g_ref, o_ref, ks_ref, vs_ref) = refs
    else:
        q_ref, kc_ref, vc_ref, lam_ref, g_ref, _, o_ref, ks_ref, vs_ref = refs
    dv = ATT_VAL_DIM

    @pl.when(pl.program_id(2) == 0)
    def _():
        vs_ref[:, dv:] = jnp.ones((vs_ref.shape[0], dv), BF16)
        ks_ref[0:n_ctx, :] = kc_ref[...]
        vs_ref[0:n_ctx, 0:dv] = vc_ref[...]
        if with_lat:
            k = _rope(kl_ref[...].astype(F32), ck_ref[...], snk_ref[...], spk_ref[...])
            ks_ref[n_ctx:, :] = k.astype(BF16)
            vs_ref[n_ctx:, 0:dv] = vl_ref[...]

    tq = q_ref.shape[0]
    q = q_ref[...].astype(F32)
    if with_lat:
        q = _rope(q, cq_ref[...], snq_ref[...], spq_ref[...])
    q = q * (math.log2(math.e) / math.sqrt(ATT_HEAD_DIM))
    first_map = lax.broadcasted_iota(jnp.int32, q.shape, 1) < ATT_HEAD_DIM
    qs = jnp.concatenate([jnp.where(first_map, q, 0.0), jnp.where(first_map, 0.0, q)], axis=0).astype(BF16)
    part = (2 * tq) // ATT_ROW_PARTS
    scores = lambda i: _dot_nt(qs[i * part:(i + 1) * part], ks_ref[...])
    s_next = scores(0)
    outs = []
    for i in range(ATT_ROW_PARTS):
        s = s_next
        if i + 1 < ATT_ROW_PARTS:
            s_next = scores(i + 1)
        p = jnp.exp2((s - jnp.max(s, axis=-1, keepdims=True)).astype(BF16))
        outs.append(_dot(p, vs_ref[...]))
    ol = jnp.concatenate(outs, axis=0)
    o = ol[:, 0:dv] / ol[:, dv:]
    lq = lam_ref[...]
    lam = (jnp.exp(jnp.sum(lq[0:1] * lq[1:2], axis=-1, keepdims=True))
           - jnp.exp(jnp.sum(lq[2:3] * lq[3:4], axis=-1, keepdims=True)) + lambda_init)
    a = o[:tq] - lam * o[tq:]
    y = a * lax.rsqrt(jnp.mean(a * a, axis=-1, keepdims=True) + SUBLN_EPS) * g_ref[...]
    o_ref[...] = (y * (1.0 - lambda_init)).astype(o_ref.dtype)


def _attention(qkv, rope_tabs, lam_vecs, subln_g, lambda_init, batch, seq, ctx_len, y_lat=None):
    m = qkv.shape[0]
    latent_queries = y_lat is None
    h_ = ATT_HEADS
    ctx_blk0 = batch * seq // ctx_len
    tq = _pick_tile(ATT_Q_TILE, seq if latent_queries else ctx_len)
    nq = (seq if latent_queries else ctx_len) // tq
    q0 = 0 if latent_queries else batch * seq // tq
    blk = lambda rows, fn: pl.BlockSpec((rows, LANES), fn)
    in_specs = [blk(tq, lambda b, h, i: (q0 + b * nq + i, h)),
                blk(ctx_len, lambda b, h, i: (ctx_blk0 + b, h_ + h)),
                blk(ctx_len, lambda b, h, i: (ctx_blk0 + b, 2 * h_ + h))]
    args = [qkv, qkv, qkv]
    n_keys = ctx_len
    if latent_queries:
        n_keys += seq
        in_specs += [blk(seq, lambda b, h, i: (b, h_ + h)), blk(seq, lambda b, h, i: (b, 2 * h_ + h))]
        in_specs += [blk(tq, lambda b, h, i: (i, 0))] * 3 + [blk(seq, lambda b, h, i: (0, 0))] * 3
        args += [qkv, qkv] + list(rope_tabs) + list(rope_tabs)
    in_specs += [pl.BlockSpec((4, ATT_HEAD_DIM), lambda b, h, i: (0, 0)),
                 pl.BlockSpec((1, ATT_VAL_DIM), lambda b, h, i: (0, 0))]
    args += [lam_vecs, subln_g.reshape(1, ATT_VAL_DIM)]
    aliases = {}
    if not latent_queries:
        in_specs.append(pl.BlockSpec(memory_space=pl.ANY))
        args.append(y_lat)
        aliases = {len(args) - 1: 0}
    return pl.pallas_call(
        functools.partial(_attn_kernel, lambda_init=lambda_init, n_ctx=ctx_len, with_lat=latent_queries),
        grid=(batch, h_, nq),
        in_specs=in_specs,
        out_specs=blk(tq, lambda b, h, i: (q0 + b * nq + i, h)),
        out_shape=jax.ShapeDtypeStruct((m, h_ * ATT_VAL_DIM), BF16),
        scratch_shapes=[pltpu.VMEM((n_keys, LANES), BF16), pltpu.VMEM((n_keys, 2 * ATT_VAL_DIM), BF16)],
        input_output_aliases=aliases,
        compiler_params=_cparams("parallel", "parallel", "arbitrary"),
        name="diff_attention_latent" if latent_queries else "diff_attention_context",
    )(*args)


def _rope_tables(seq):
    t = jnp.arange(seq, dtype=jnp.int32)
    row = (t // GRID_W).astype(F32)
    col = (t % GRID_W).astype(F32)
    n_pairs = ATT_HEAD_DIM // 4
    inv_freq = ROPE_BASE ** (-jnp.arange(n_pairs, dtype=F32) / n_pairs)
    ang = jnp.concatenate([row[:, None] * inv_freq, col[:, None] * inv_freq], axis=-1)
    cos = jnp.repeat(jnp.cos(ang), 2, axis=-1)
    sin = jnp.repeat(jnp.sin(ang), 2, axis=-1)
    even = (jnp.arange(ATT_HEAD_DIM) % 2 == 0)[None, :]
    both = lambda a: jnp.concatenate([a, a], axis=-1)
    return both(cos), both(jnp.where(even, -sin, 0.0)), both(jnp.where(even, 0.0, sin))


def _rwkv_prep_kernel(z_ref, zp_ref, zn_ref, smat_ref, sh_ref, w2_ref, a2_ref, g2_ref, w0_ref, a0_ref,
                      kkw_ref, ka_ref, rk_ref, seg_ref,
                      r_o, v_o, kk_o, g_o, bonus_o, lw_o, kd_o, b_o,
                      *, tile, n_lat_rows, seq, ctx_len):
    first, last = _seq_edges(pl.program_id(0), tile, n_lat_rows, seq, ctx_len)
    hd = RWKV_DIM
    zc = z_ref[...]
    shifted = _dot(smat_ref[...], zc)
    prev_row = jnp.where(first, 0.0, zp_ref[...].astype(F32)[HALO - 1:HALO, :])
    next_row = jnp.where(last, 0.0, zn_ref[...].astype(F32)[0:1, :])
    t_idx = lax.broadcasted_iota(jnp.int32, (tile, 1), 0)
    z = (sh_ref[0:1, :] * jnp.where(t_idx == 0, prev_row, shifted[:tile])
         + sh_ref[1:2, :] * zc.astype(F32)
         + sh_ref[2:3, :] * jnp.where(t_idx == tile - 1, next_row, shifted[tile:]))
    seg = seg_ref[...]
    r = z[:, 0:hd]
    k = z[:, hd:2 * hd]
    v = z[:, 2 * hd:3 * hd]
    o = 3 * hd
    wd = z[:, o:o + 2 * DECAY_LORA]
    ad = z[:, o + 2 * DECAY_LORA:o + 2 * DECAY_LORA + 2 * ICLR_LORA]
    gd = z[:, o + 2 * DECAY_LORA + 2 * ICLR_LORA:]
    kkf = k * kkw_ref[...]
    kk = kkf * lax.rsqrt(jnp.maximum(_seg_sum(kkf * kkf, seg), 1e-24))
    g = _dot(jax.nn.sigmoid(gd).astype(BF16), g2_ref[...])
    wl = w0_ref[...] + _dot(jnp.tanh(wd).astype(BF16), w2_ref[...])
    al = a0_ref[...] + _dot(ad.astype(BF16), a2_ref[...])
    softplus = jnp.maximum(-wl, 0.0) + jnp.log(1.0 + jnp.exp(-jnp.abs(wl)))
    lw = -jnp.exp(-softplus - 0.5)
    a = jax.nn.sigmoid(al)
    r_o[...] = r.astype(r_o.dtype)
    v_o[...] = v.astype(v_o.dtype)
    kk_o[...] = kk.astype(kk_o.dtype)
    g_o[...] = g.astype(g_o.dtype)
    lw_o[...] = lw
    rk = rk_ref[...]
    bonus = jnp.zeros_like(r)
    for d in range(2):
        a_d = a[:, d * hd:(d + 1) * hd]
        kd = k * (1.0 + (a_d - 1.0) * ka_ref[...])
        kd_o[:, d * hd:(d + 1) * hd] = kd.astype(kd_o.dtype)
        b_o[:, d * hd:(d + 1) * hd] = (kk * a_d).astype(b_o.dtype)
        bonus = bonus + _seg_sum(r * kd * rk, seg) * v
    bonus_o[...] = bonus.astype(bonus_o.dtype)


def _block_diag2(w):
    z = jnp.zeros_like(w[0])
    return jnp.concatenate([jnp.concatenate([w[0], z], axis=1), jnp.concatenate([z, w[1]], axis=1)], axis=0)


def _head_seg_matrix(n, head):
    idx = jnp.arange(n) // head
    return (idx[:, None] == idx[None, :]).astype(BF16)


def _rwkv_prep(zr, lp, tile, n_lat_rows, seq, ctx_len):
    m, width = zr.shape
    hd = RWKV_DIM
    full = lambda a: pl.BlockSpec(a.shape, lambda i: (0,) * a.ndim)
    shift_mat = jnp.concatenate([jnp.eye(tile, k=-1, dtype=BF16), jnp.eye(tile, k=1, dtype=BF16)], axis=0)
    consts = [shift_mat, lp['rwkv_shift'],
              _block_diag2(lp['rwkv_w2']).astype(BF16), _block_diag2(lp['rwkv_a2']).astype(BF16),
              lp['rwkv_g2'].astype(BF16),
              lp['rwkv_w0'].reshape(1, 2 * hd), lp['rwkv_a0'].reshape(1, 2 * hd),
              lp['rwkv_kk'].reshape(1, hd), lp['rwkv_ka'].reshape(1, hd), lp['rwkv_rk'].reshape(1, hd),
              _head_seg_matrix(hd, RWKV_HEAD_DIM)]
    one = jax.ShapeDtypeStruct((m, hd), BF16)
    two = jax.ShapeDtypeStruct((m, 2 * hd), BF16)
    ospec = lambda n: pl.BlockSpec((tile, n), lambda i: (i, 0))
    return pl.pallas_call(
        functools.partial(_rwkv_prep_kernel, tile=tile, n_lat_rows=n_lat_rows, seq=seq, ctx_len=ctx_len),
        grid=(m // tile,),
        in_specs=_halo_specs(tile, width, m) + [full(a) for a in consts],
        out_specs=[ospec(hd)] * 5 + [ospec(2 * hd)] * 3,
        out_shape=[one] * 5 + [jax.ShapeDtypeStruct((m, 2 * hd), F32), two, two],
        compiler_params=_cparams("parallel"),
        name="rwkv_prep",
    )(zr, zr, zr, *consts)


def _split3(x):
    h1 = x.astype(BF16)
    r1 = x - h1.astype(F32)
    h2 = r1.astype(BF16)
    h3 = (r1 - h2.astype(F32)).astype(BF16)
    return h1, h2, h3


def _scan_kernel(*refs):
    ins = (refs[0:6], refs[6:12])
    y_refs = refs[12:14]
    s_ref = refs[14]
    c_len = SCAN_CHUNK
    n_sub = refs[0].shape[0] // c_len
    n_pairs = RWKV_DIM // LANES

    @pl.when(pl.program_id(1) == 0)
    def _():
        s_ref[...] = jnp.zeros_like(s_ref)

    row = lax.broadcasted_iota(jnp.int32, (c_len, c_len), 0)
    col = lax.broadcasted_iota(jnp.int32, (c_len, c_len), 1)
    row2 = lax.broadcasted_iota(jnp.int32, (c_len, 2 * c_len), 0)
    col2 = lax.broadcasted_iota(jnp.int32, (c_len, 2 * c_len), 1)
    second_half = col2 >= c_len
    col2 = jnp.where(second_half, col2 - c_len, col2)
    eye = jnp.where(row == col, 1.0, 0.0)
    lane = lax.broadcasted_iota(jnp.int32, (c_len, LANES), 1)
    head0 = lane < RWKV_HEAD_DIM
    srow = lax.broadcasted_iota(jnp.int32, (LANES, LANES), 0) // RWKV_HEAD_DIM
    scol = lax.broadcasted_iota(jnp.int32, (LANES, LANES), 1) // RWKV_HEAD_DIM
    same_head = srow == scol

    masks, ops = [], {}
    for d in range(2):
        ahead = (row - col) if d == 0 else (col - row)
        ahead2 = (row2 - col2) if d == 0 else (col2 - row2)
        strict = ahead > 0
        tri = jnp.where(ahead >= 0, 1.0, 0.0).astype(BF16)
        pair_masks = []
        for sh in range(int(math.log2(c_len))):
            same_pair = (row >> (sh + 1)) == (col >> (sh + 1))
            pair_masks.append(same_pair & ((row >> sh) != (col >> sh)) & strict)
        masks.append(dict(pair=pair_masks,
                          ak=second_half & (ahead2 > 0),
                          y=ahead2 >= 0))
        r_ref, lw_ref, k_ref, v_ref, kk_ref, b_ref = ins[d]
        for g in range(n_sub):
            rows = pl.ds(g * c_len, c_len)
            lw = lw_ref[rows, :]
            h1, h2, h3 = _split3(lw)
            c = _dot(tri, h1) + _dot(tri, h2) + _dot(tri, h3)
            tot = jnp.sum(lw, axis=0, keepdims=True)
            e_neg = jnp.exp(-c)
            dec = jnp.exp(tot - c)
            f32 = lambda ref: ref[rows, :].astype(F32)
            b_f, k_f = f32(b_ref), f32(k_ref)
            ops[d, g] = dict(
                a_t=-f32(kk_ref) * jnp.exp(c - lw), r_t=f32(r_ref) * jnp.exp(c),
                b_t=(b_f * e_neg).astype(BF16), k_t=(k_f * e_neg).astype(BF16),
                b_p=(b_f * dec).astype(BF16), k_p=(k_f * dec).astype(BF16),
                p_tot=jnp.exp(tot), v=v_ref[rows, :])

    groups = [(d, g, p) for d in range(2) for g in range(n_sub) for p in range(n_pairs)]
    chains = [gr + (hh,) for gr in groups for hh in range(2)]
    lanes_of = lambda p: slice(p * LANES, (p + 1) * LANES)

    scores, keep_of = {}, {}
    for ch in chains:
        d, g, p, hh = ch
        sl = lanes_of(p)
        keep_of[ch] = head0 if hh == 0 else jnp.logical_not(head0)
        lhs = jnp.concatenate([jnp.where(keep_of[ch], ops[d, g]['a_t'][:, sl], 0.0),
                               jnp.where(keep_of[ch], ops[d, g]['r_t'][:, sl], 0.0)], axis=0).astype(BF16)
        rhs = jnp.concatenate([ops[d, g]['b_t'][:, sl], ops[d, g]['k_t'][:, sl]], axis=0)
        scores[ch] = _dot_nt(lhs, rhs)
    vv = {}
    for gr in groups:
        d, g, p = gr
        v_p = ops[d, g]['v'][:, lanes_of(p)]
        vv[gr] = jnp.concatenate([v_p, v_p], axis=0)
    w0, l_mat, x_inv = {}, {}, {}
    for ch in chains:
        d, g, p, hh = ch
        top = scores[ch][:c_len]
        w0[ch] = _dot(jnp.where(masks[d]['ak'], top, 0.0).astype(BF16), vv[d, g, p])
        l_mat[ch] = top[:, :c_len]
        x_inv[ch] = eye + jnp.where(masks[d]['pair'][0], l_mat[ch], 0.0)

    for level in range(1, int(math.log2(c_len))):
        t1 = {}
        for ch in chains:
            pm = masks[ch[0]]['pair'][level]
            t1[ch] = _dot(jnp.where(pm, l_mat[ch], 0.0).astype(BF16), x_inv[ch].astype(BF16))
        for ch in chains:
            x_inv[ch] = x_inv[ch] + _dot(x_inv[ch].astype(BF16), t1[ch].astype(BF16))
    xa_h, u0_h = {}, {}
    for ch in chains:
        d, g, p, hh = ch
        xb = x_inv[ch].astype(BF16)
        xa_h[ch] = _dot(xb, jnp.where(keep_of[ch], ops[d, g]['a_t'][:, lanes_of(p)], 0.0).astype(BF16))
        u0_h[ch] = _dot(xb, w0[ch].astype(BF16))
    p_rows = [ops[d, g]['p_tot'] for d in range(2) for g in range(n_sub)]
    p_all = jnp.concatenate(p_rows + [jnp.zeros((LANES - len(p_rows), RWKV_DIM), F32)], axis=0)
    p_t = [jnp.transpose(p_all[:, lanes_of(p)]) for p in range(n_pairs)]
    bk_t = {}
    for d, g, p in groups:
        sl = lanes_of(p)
        bk_t[d, g, p] = jnp.transpose(jnp.concatenate([ops[d, g]['b_p'][:, sl], ops[d, g]['k_p'][:, sl]], axis=0))
    lhs_t, u0_pair, p_col = {}, {}, {}
    for gr in groups:
        d, g, p = gr
        xa = (xa_h[gr + (0,)] + xa_h[gr + (1,)]).astype(BF16)
        u0_pair[gr] = jnp.where(head0, u0_h[gr + (0,)], u0_h[gr + (1,)])
        lhs_t[gr] = jnp.concatenate([xa, ops[d, g]['r_t'][:, lanes_of(p)].astype(BF16)], axis=0)
        r_idx = d * n_sub + g
        p_col[gr] = jnp.broadcast_to(p_t[p][:, r_idx:r_idx + 1], (LANES, LANES))

    state = {(d, p): s_ref[d, p] for d in range(2) for p in range(n_pairs)}
    for step in range(n_sub):
        sub_of = lambda d: step if d == 0 else n_sub - 1 - step
        live = [(d, sub_of(d), p) for d in range(2) for p in range(n_pairs)]
        prod = {gr: _dot(lhs_t[gr], state[gr[0], gr[2]].astype(BF16)) for gr in live}
        uv = {gr: jnp.concatenate([(prod[gr][:c_len] + u0_pair[gr]).astype(BF16), vv[gr][:c_len]], axis=0)
              for gr in live}
        upd = {gr: _dot(bk_t[gr], uv[gr]) for gr in live}
        for gr in live:
            d, g, p = gr
            state[d, p] = p_col[gr] * state[d, p] + jnp.where(same_head, upd[gr], 0.0)
        for gr in live:
            d, g, p = gr
            ys = [_dot(jnp.where(masks[d]['y'], scores[gr + (hh,)][c_len:], 0.0).astype(BF16), uv[gr])
                  for hh in range(2)]
            y_refs[d][pl.ds(g * c_len, c_len), lanes_of(p)] = prod[gr][c_len:] + jnp.where(head0, ys[0], ys[1])
    for (d, p), s_new in state.items():
        s_ref[d, p] = s_new


def _rwkv_scan(r, lw, kd, v, kk, bvec, batch, seq, ctx_len):
    m, hd = r.shape
    n_sub = _pick_tile(SCAN_SUBCHUNKS, ctx_len // SCAN_CHUNK, seq // SCAN_CHUNK)
    c_len = n_sub * SCAN_CHUNK
    n_ctx, n_lat = ctx_len // c_len, seq // c_len
    ctx0 = batch * seq // c_len

    def rowblk(d):
        def fn(b, j):
            jc = j if d == 0 else n_ctx - 1 - j
            jl = (j - n_ctx) if d == 0 else n_lat - 1 - (j - n_ctx)
            return jnp.where(j < n_ctx, ctx0 + b * n_ctx + jc, b * n_lat + jl)
        return fn

    in_specs, args = [], []
    for d in range(2):
        shared = pl.BlockSpec((c_len, hd), lambda b, j, f=rowblk(d): (f(b, j), 0))
        per_dir = pl.BlockSpec((c_len, hd), lambda b, j, f=rowblk(d), d=d: (f(b, j), d))
        in_specs += [shared, per_dir, per_dir, shared, shared, per_dir]
        args += [r, lw, kd, v, kk, bvec]
    out = jax.ShapeDtypeStruct((m, hd), F32)
    return pl.pallas_call(
        _scan_kernel,
        grid=(batch, n_ctx + n_lat),
        in_specs=in_specs,
        out_specs=[pl.BlockSpec((c_len, hd), lambda b, j, f=rowblk(d): (f(b, j), 0)) for d in range(2)],
        out_shape=[out, out],
        scratch_shapes=[pltpu.VMEM((2, hd // LANES, LANES, LANES), F32)],
        compiler_params=_cparams("parallel", "arbitrary"),
        name="rwkv_scan",
    )(*args)


def _merge_kernel(x_ref, yc_ref, ya_ref, yf_ref, yb_ref, bonus_ref, rg_ref, gg_ref, gb_ref, seg_ref,
                  gt_ref, pc_ref, pa_ref, pr_ref, wo_ref, gate_ref, o_ref):
    d = x_ref.shape[1]
    seg = seg_ref[...]
    ys = yf_ref[...] + yb_ref[...]
    inv_n = 1.0 / RWKV_HEAD_DIM
    mu = _seg_sum(ys, seg) * inv_n
    dlt = ys - mu
    var = _seg_sum(dlt * dlt, seg) * inv_n
    yn = dlt * lax.rsqrt(var + GN_EPS) * gg_ref[...] + gb_ref[...]
    y_rw = ((yn + bonus_ref[...].astype(F32)) * rg_ref[...].astype(F32)).astype(BF16)
    g = jax.nn.sigmoid(gt_ref[...].astype(F32))
    mix = (g[:, 0:d] * _dot(yc_ref[...], pc_ref[...])
           + g[:, d:2 * d] * _dot(ya_ref[...], pa_ref[...])
           + g[:, 2 * d:] * _dot(y_rw, pr_ref[...]))
    o_ref[...] = x_ref[...] + gate_ref[...] * _dot(mix.astype(BF16), wo_ref[...])


def _merge(x, rows, yc, ya, yf, yb, bonus, rg, gn_g, gn_b, gates, pc, pa, pr, wo, mod_tab, tm, row_of_tile):
    d = x.shape[1]
    hd = yf.shape[1]
    seg = _head_seg_matrix(hd, RWKV_HEAD_DIM)
    gn_g, gn_b = gn_g.reshape(1, hd), gn_b.reshape(1, hd)
    tok = lambda a: pl.BlockSpec((tm, a.shape[1]), lambda i: (i, 0))
    full = lambda a: pl.BlockSpec(a.shape, lambda i: (0, 0))
    return pl.pallas_call(
        _merge_kernel,
        grid=(rows // tm,),
        in_specs=[tok(x), tok(yc), tok(ya), tok(yf), tok(yb), tok(bonus), tok(rg), full(gn_g), full(gn_b),
                  full(seg), tok(gates), full(pc), full(pa), full(pr), full(wo),
                  pl.BlockSpec((None, None, 1, d), lambda i: (row_of_tile(i), 2, 0, 0))],
        out_specs=tok(x),
        out_shape=jax.ShapeDtypeStruct((rows, d), F32),
        compiler_params=_cparams("parallel"),
        name="merge_out_proj",
    )(x, yc, ya, yf, yb, bonus, rg, gn_g, gn_b, seg, gates, pc, pa, pr, wo, mod_tab)


def _mlp_kernel(*refs, final):
    if final:
        x_ref, g_ref, sh_ref, sc_ref, gate_ref, w1_ref, w2_ref, fg_ref, o_ref, h_ref, acc_ref = refs
    else:
        x_ref, g_ref, sh_ref, sc_ref, gate_ref, w1_ref, w2_ref, o_ref, h_ref, acc_ref = refs
    f = pl.program_id(1)

    @pl.when(f == 0)
    def _():
        h_ref[...] = _modulated_norm(x_ref[...], g_ref[...], sh_ref[...], sc_ref[...]).astype(BF16)
        acc_ref[...] = jnp.zeros_like(acc_ref)

    t = jnp.maximum(_dot(h_ref[...], w1_ref[...]), 0.0)
    acc_ref[...] += _dot((t * t).astype(BF16), w2_ref[...])

    @pl.when(f == pl.num_programs(1) - 1)
    def _():
        y = x_ref[...] + gate_ref[...] * acc_ref[...]
        if final:
            y = y * lax.rsqrt(jnp.mean(y * y, axis=-1, keepdims=True) + NORM_EPS) * fg_ref[...]
        o_ref[...] = y


def _mlp(x, rows, gain, mod_tab, w1, w2, final_g, tm, tf, row_of_tile):
    d = x.shape[1]
    dff = w1.shape[1]
    final = final_g is not None
    tab = lambda which: pl.BlockSpec((None, None, 1, d), lambda i, f: (row_of_tile(i), which, 0, 0))
    in_specs = [pl.BlockSpec((tm, d), lambda i, f: (i, 0)),
                pl.BlockSpec((1, d), lambda i, f: (0, 0)),
                tab(3), tab(4), tab(5),
                pl.BlockSpec((d, tf), lambda i, f: (0, f)),
                pl.BlockSpec((tf, d), lambda i, f: (f, 0))]
    args = [x, gain.reshape(1, d), mod_tab, mod_tab, mod_tab, w1, w2]
    if final:
        in_specs.append(pl.BlockSpec((1, d), lambda i, f: (0, 0)))
        args.append(final_g.reshape(1, d))
    return pl.pallas_call(
        functools.partial(_mlp_kernel, final=final),
        grid=(rows // tm, dff // tf),
        in_specs=in_specs,
        out_specs=pl.BlockSpec((tm, d), lambda i, f: (i, 0)),
        out_shape=jax.ShapeDtypeStruct((rows, d), F32),
        scratch_shapes=[pltpu.VMEM((tm, d), BF16), pltpu.VMEM((tm, d), F32)],
        compiler_params=_cparams("parallel", "arbitrary"),
        name="mlp_final" if final else "mlp",
    )(*args)


def kernel(x, c, ctx, c_ctx, mod_w, mod_b, norm1_g, norm2_g, w_in, conv_dw_w, conv_dw_b, conv_ln_g, conv_ln_b, p_conv, att_lq1, att_lk1, att_lq2, att_lk2, att_subln_g, p_att, rwkv_shift, rwkv_w0, rwkv_w2, rwkv_a0, rwkv_a2, rwkv_g2, rwkv_kk, rwkv_ka, rwkv_rk, rwkv_gn_g, rwkv_gn_b, p_rwkv, w_out, mlp_w1, mlp_w2, final_g):
    batch, seq, d = x.shape
    ctx_len = ctx.shape[1]
    depth = mod_w.shape[0]
    n_lat = batch * seq
    conv_w = 2 * conv_dw_w.shape[2]
    att_w = 3 * ATT_HEADS * ATT_VAL_DIM
    rw_w = 3 * RWKV_DIM + 2 * DECAY_LORA + 2 * ICLR_LORA + GATE_LORA
    assert w_in.shape[2] == conv_w + att_w + rw_w + 3 * d
    assert seq % SCAN_CHUNK == 0 and ctx_len % SCAN_CHUNK == 0 and (batch * seq) % ctx_len == 0

    tm = _pick_tile(1024, seq, batch * ctx_len)
    tmm = _pick_tile(512, seq, batch * ctx_len)
    tc = _pick_tile(256, seq, ctx_len)
    tiles_per_seq = seq // tm
    row_of_tile = lambda i: jnp.minimum(i // tiles_per_seq, batch)
    row_of_tile_m = lambda i: jnp.minimum(i // (seq // tmm), batch)

    xs = jnp.concatenate([x.reshape(n_lat, d), ctx.reshape(batch * ctx_len, d)], axis=0)
    pad = (-(batch + 1)) % 8
    c_all = jnp.concatenate([c, c_ctx[None, :], jnp.zeros((pad, d), F32)], axis=0)
    rope_tabs = _rope_tables(seq)

    for l in range(depth):
        last = l == depth - 1
        lambda_init = 0.8 - 0.6 * math.exp(-0.3 * l)
        mod = _modulation(c_all, mod_w[l], mod_b[l])
        mod_tab = mod[:batch + 1].reshape(batch + 1, 6, 1, d)

        wl = w_in[l].astype(BF16)
        o1, o2, o3 = conv_w, conv_w + att_w, conv_w + att_w + rw_w
        proj = functools.partial(_project, xs, norm1_g[l], mod_tab, 0, tm=tm, row_of_tile=row_of_tile)
        zc = proj(wl[:, :o1], BF16, tn=o1, name="proj_conv")
        qkv = proj(wl[:, o1:o2], BF16, tn=att_w, name="proj_qkv")
        zr = proj(wl[:, o2:o3], BF16, tn=rw_w, name="proj_rwkv")
        gates = proj(wl[:, o3:], BF16, tn=3 * d, name="proj_gates")

        y_conv = _conv_branch(zc, conv_dw_w[l], conv_dw_b[l], conv_ln_g[l], conv_ln_b[l],
                              tc, n_lat, seq, ctx_len)

        lam_vecs = jnp.stack([att_lq1[l], att_lk1[l], att_lq2[l], att_lk2[l]])
        y_att = _attention(qkv, rope_tabs, lam_vecs, att_subln_g[l], lambda_init, batch, seq, ctx_len)
        if not last:
            y_att = _attention(qkv, None, lam_vecs, att_subln_g[l], lambda_init, batch, seq, ctx_len, y_att)

        lp = dict(rwkv_shift=rwkv_shift[l], rwkv_w0=rwkv_w0[l], rwkv_w2=rwkv_w2[l], rwkv_a0=rwkv_a0[l],
                  rwkv_a2=rwkv_a2[l], rwkv_g2=rwkv_g2[l], rwkv_kk=rwkv_kk[l], rwkv_ka=rwkv_ka[l],
                  rwkv_rk=rwkv_rk[l])
        r, v, kk, g, bonus, lw, kd, bvec = _rwkv_prep(zr, lp, tc, n_lat, seq, ctx_len)
        yf, yb = _rwkv_scan(r, lw, kd, v, kk, bvec, batch, seq, ctx_len)

        rows = n_lat if last else xs.shape[0]
        xs = _merge(xs, rows, y_conv, y_att, yf, yb, bonus, g, rwkv_gn_g[l], rwkv_gn_b[l], gates,
                    p_conv[l].astype(BF16), p_att[l].astype(BF16),
                    p_rwkv[l].astype(BF16), w_out[l].astype(BF16), mod_tab, tmm, row_of_tile_m)
        xs = _mlp(xs, rows, norm2_g[l], mod_tab, mlp_w1[l].astype(BF16), mlp_w2[l].astype(BF16),
                  final_g if last else None, tm, _pick_tile(512, mlp_w1.shape[2]), row_of_tile)
    return xs.reshape(batch, seq, d)
```

```python
import functools
import math

import jax
import jax.numpy as jnp
from jax import lax
from jax.experimental import pallas as pl
from jax.experimental.pallas import tpu as pltpu

F32 = jnp.float32
BF16 = jnp.bfloat16

GRID_W = 64
NORM_EPS = 1e-6
CONV_WIDTH = 31
LN_EPS = 1e-5
ATT_HEADS = 4
ATT_HEAD_DIM = 64
ATT_VAL_DIM = 2 * ATT_HEAD_DIM
ROPE_BASE = 10000.0
SUBLN_EPS = 1e-5
RWKV_HEADS = 8
RWKV_HEAD_DIM = 64
RWKV_DIM = RWKV_HEADS * RWKV_HEAD_DIM
DECAY_LORA = 64
ICLR_LORA = 64
GATE_LORA = 128
GN_EPS = 64e-5

LANES = 128
SUBLANES = 8
HALO = 16
SCAN_CHUNK = 64
SCAN_SUBCHUNKS = 4
SCAN_WAVE = 2
ATT_Q_TILE = 512
ATT_ROW_PARTS = 4
VMEM_LIMIT = 56 << 20


def _cparams(*sem):
    return pltpu.CompilerParams(dimension_semantics=sem, vmem_limit_bytes=VMEM_LIMIT)


def _pick_tile(limit, *sizes):
    t = limit
    while any(s % t for s in sizes):
        t //= 2
    return t


def _dot(a, b):
    return jnp.dot(a, b, preferred_element_type=F32)


def _dot_nt(a, b):
    return lax.dot_general(a, b, (((1,), (1,)), ((), ())), preferred_element_type=F32)


def _dot_tn(a, b):
    return lax.dot_general(a, b, (((0,), (0,)), ((), ())), preferred_element_type=F32)


def _seg_sum(x, seg):
    hi = x.astype(BF16)
    lo = (x - hi.astype(F32)).astype(BF16)
    return _dot(hi, seg) + _dot(lo, seg)


def _modulated_norm(x, gain, shift, scale):
    y = x * lax.rsqrt(jnp.mean(x * x, axis=-1, keepdims=True) + NORM_EPS)
    return (y * gain) * (1.0 + scale) + shift


def _mod_kernel(c_ref, w_ref, b_ref, o_ref):
    c = c_ref[...]
    s = c * jax.nn.sigmoid(c)
    o_ref[...] = _dot(s.astype(BF16), w_ref[...].astype(BF16)) + b_ref[...]


def _modulation(c_all, mod_w, mod_b):
    rows, d = c_all.shape
    n = mod_w.shape[1]
    tn = _pick_tile(1024, n // 6)
    return pl.pallas_call(
        _mod_kernel,
        grid=(n // tn,),
        in_specs=[pl.BlockSpec((rows, d), lambda j: (0, 0)),
                  pl.BlockSpec((d, tn), lambda j: (0, j)),
                  pl.BlockSpec((1, tn), lambda j: (0, j))],
        out_specs=pl.BlockSpec((rows, tn), lambda j: (0, j)),
        out_shape=jax.ShapeDtypeStruct((rows, n), F32),
        compiler_params=_cparams("arbitrary"),
        name="adaln_modulation",
    )(c_all, mod_w, mod_b.reshape(1, n))


def _proj_kernel(x_ref, g_ref, sh_ref, sc_ref, w_ref, o_ref, h_ref):
    @pl.when(pl.program_id(1) == 0)
    def _():
        h_ref[...] = _modulated_norm(x_ref[...], g_ref[...], sh_ref[...], sc_ref[...]).astype(BF16)

    o_ref[...] = _dot(h_ref[...], w_ref[...]).astype(o_ref.dtype)


def _project(x, gain, mod_tab, which_shift, w, out_dtype, tm, tn, row_of_tile, name):
    m, d = x.shape
    n = w.shape[1]

    def tab_spec(which):
        return pl.BlockSpec((None, None, 1, d), lambda i, j: (row_of_tile(i), which, 0, 0))

    return pl.pallas_call(
        _proj_kernel,
        grid=(m // tm, n // tn),
        in_specs=[pl.BlockSpec((tm, d), lambda i, j: (i, 0)),
                  pl.BlockSpec((1, d), lambda i, j: (0, 0)),
                  tab_spec(which_shift), tab_spec(which_shift + 1),
                  pl.BlockSpec((d, tn), lambda i, j: (0, j))],
        out_specs=pl.BlockSpec((tm, tn), lambda i, j: (i, j)),
        out_shape=jax.ShapeDtypeStruct((m, n), out_dtype),
        scratch_shapes=[pltpu.VMEM((tm, d), BF16)],
        compiler_params=_cparams("parallel", "arbitrary"),
        name=name,
    )(x, gain.reshape(1, d), mod_tab, mod_tab, w)


def _seq_edges(i, tile, n_lat_rows, seq, ctx_len):
    n_lat_tiles = n_lat_rows // tile
    is_lat = i < n_lat_tiles
    per = jnp.where(is_lat, seq // tile, ctx_len // tile)
    pos = jnp.where(is_lat, i, i - n_lat_tiles) % per
    return pos == 0, pos == per - 1


def _conv_kernel(z_ref, zp_ref, zn_ref, w_ref, b_ref, lg_ref, lb_ref, o_ref, ext_ref, rot_ref,
                 *, tile, n_lat_rows, seq, ctx_len):
    first, last = _seq_edges(pl.program_id(0), tile, n_lat_rows, seq, ctx_len)
    cd = o_ref.shape[1]

    def glu(z):
        z = z.astype(F32)
        return z[:, :cd] * jax.nn.sigmoid(z[:, cd:])

    ext_ref[0:HALO, :] = jnp.where(first, 0.0, glu(zp_ref[...]))
    ext_ref[HALO:HALO + tile, :] = glu(z_ref[...])
    ext_ref[HALO + tile:, :] = jnp.where(last, 0.0, glu(zn_ref[...]))
    span = rot_ref.shape[1]
    for r in range(SUBLANES):
        rot_ref[r] = ext_ref[pl.ds(r, span), :]
    half = CONV_WIDTH // 2
    acc = jnp.zeros((tile, cd), F32)
    for k in range(CONV_WIDTH):
        start = HALO - half + k
        acc = acc + w_ref[k:k + 1, :] * rot_ref[start % SUBLANES, pl.ds(start - start % SUBLANES, tile), :]
    u = acc + b_ref[...]
    mu = jnp.mean(u, axis=-1, keepdims=True)
    dlt = u - mu
    var = jnp.mean(dlt * dlt, axis=-1, keepdims=True)
    y = dlt * lax.rsqrt(var + LN_EPS) * lg_ref[...] + lb_ref[...]
    o_ref[...] = (y * jax.nn.sigmoid(y)).astype(o_ref.dtype)


def _halo_specs(tile, width, m):
    per = tile // HALO
    last_blk = m // HALO - 1
    return [pl.BlockSpec((tile, width), lambda i: (i, 0)),
            pl.BlockSpec((HALO, width), lambda i: (jnp.maximum(i * per - 1, 0), 0)),
            pl.BlockSpec((HALO, width), lambda i: (jnp.minimum((i + 1) * per, last_blk), 0))]


def _conv_branch(zc, dw_w, dw_b, ln_g, ln_b, tile, n_lat_rows, seq, ctx_len):
    m, width = zc.shape
    cd = width // 2
    vec = pl.BlockSpec((1, cd), lambda i: (0, 0))
    return pl.pallas_call(
        functools.partial(_conv_kernel, tile=tile, n_lat_rows=n_lat_rows, seq=seq, ctx_len=ctx_len),
        grid=(m // tile,),
        in_specs=_halo_specs(tile, width, m) + [pl.BlockSpec((CONV_WIDTH, cd), lambda i: (0, 0)),
                                                vec, vec, vec],
        out_specs=pl.BlockSpec((tile, cd), lambda i: (i, 0)),
        out_shape=jax.ShapeDtypeStruct((m, cd), BF16),
        scratch_shapes=[pltpu.VMEM((tile + 2 * HALO, cd), F32),
                        pltpu.VMEM((SUBLANES, tile + 2 * HALO - SUBLANES, cd), F32)],
        compiler_params=_cparams("parallel"),
        name="conv_branch",
    )(zc, zc, zc, dw_w, dw_b.reshape(1, cd), ln_g.reshape(1, cd), ln_b.reshape(1, cd))


def _rope(x, cos, sin_next, sin_prev):
    return x * cos + pltpu.roll(x, LANES - 1, 1) * sin_next + pltpu.roll(x, 1, 1) * sin_prev


def _attn_kernel(*refs, lambda_init, n_ctx, with_lat):
    if with_lat:
        (q_ref, kc_ref, vc_ref, kl_ref, vl_ref, cq_ref, snq_ref, spq_ref, ck_ref, snk_ref, spk_ref,
         lam_ref, g_ref, o_ref, ks_ref, vs_ref) = refs
    else:
        q_ref, kc_ref, vc_ref, lam_ref, g_ref, _, o_ref, ks_ref, vs_ref = refs
    dv = ATT_VAL_DIM

    @pl.when(pl.program_id(2) == 0)
    def _():
        vs_ref[:, dv:] = jnp.ones((vs_ref.shape[0], dv), BF16)
        ks_ref[0:n_ctx, :] = kc_ref[...]
        vs_ref[0:n_ctx, 0:dv] = vc_ref[...]
        if with_lat:
            k = _rope(kl_ref[...].astype(F32), ck_ref[...], snk_ref[...], spk_ref[...])
            ks_ref[n_ctx:, :] = k.astype(BF16)
            vs_ref[n_ctx:, 0:dv] = vl_ref[...]

    tq = q_ref.shape[0]
    q = q_ref[...].astype(F32)
    if with_lat:
        q = _rope(q, cq_ref[...], snq_ref[...], spq_ref[...])
    q = q * (math.log2(math.e) / math.sqrt(ATT_HEAD_DIM))
    first_map = lax.broadcasted_iota(jnp.int32, q.shape, 1) < ATT_HEAD_DIM
    qs = jnp.concatenate([jnp.where(first_map, q, 0.0), jnp.where(first_map, 0.0, q)], axis=0).astype(BF16)
    part = (2 * tq) // ATT_ROW_PARTS
    scores = lambda i: _dot_nt(qs[i * part:(i + 1) * part], ks_ref[...])
    s_next = scores(0)
    outs = []
    for i in range(ATT_ROW_PARTS):
        s = s_next
        if i + 1 < ATT_ROW_PARTS:
            s_next = scores(i + 1)
        p = jnp.exp2((s - jnp.max(s, axis=-1, keepdims=True)).astype(BF16))
        outs.append(_dot(p, vs_ref[...]))
    ol = jnp.concatenate(outs, axis=0)
    o = ol[:, 0:dv] / ol[:, dv:]
    lq = lam_ref[...]
    lam = (jnp.exp(jnp.sum(lq[0:1] * lq[1:2], axis=-1, keepdims=True))
           - jnp.exp(jnp.sum(lq[2:3] * lq[3:4], axis=-1, keepdims=True)) + lambda_init)
    a = o[:tq] - lam * o[tq:]
    y = a * lax.rsqrt(jnp.mean(a * a, axis=-1, keepdims=True) + SUBLN_EPS) * g_ref[...]
    o_ref[...] = (y * (1.0 - lambda_init)).astype(o_ref.dtype)


def _attention(qkv, rope_tabs, lam_vecs, subln_g, lambda_init, batch, seq, ctx_len, y_lat=None):
    m = qkv.shape[0]
    latent_queries = y_lat is None
    h_ = ATT_HEADS
    ctx_blk0 = batch * seq // ctx_len
    tq = _pick_tile(ATT_Q_TILE, seq if latent_queries else ctx_len)
    nq = (seq if latent_queries else ctx_len) // tq
    q0 = 0 if latent_queries else batch * seq // tq
    blk = lambda rows, fn: pl.BlockSpec((rows, LANES), fn)
    in_specs = [blk(tq, lambda b, h, i: (q0 + b * nq + i, h)),
                blk(ctx_len, lambda b, h, i: (ctx_blk0 + b, h_ + h)),
                blk(ctx_len, lambda b, h, i: (ctx_blk0 + b, 2 * h_ + h))]
    args = [qkv, qkv, qkv]
    n_keys = ctx_len
    if latent_queries:
        n_keys += seq
        in_specs += [blk(seq, lambda b, h, i: (b, h_ + h)), blk(seq, lambda b, h, i: (b, 2 * h_ + h))]
        in_specs += [blk(tq, lambda b, h, i: (i, 0))] * 3 + [blk(seq, lambda b, h, i: (0, 0))] * 3
        args += [qkv, qkv] + list(rope_tabs) + list(rope_tabs)
    in_specs += [pl.BlockSpec((4, ATT_HEAD_DIM), lambda b, h, i: (0, 0)),
                 pl.BlockSpec((1, ATT_VAL_DIM), lambda b, h, i: (0, 0))]
    args += [lam_vecs, subln_g.reshape(1, ATT_VAL_DIM)]
    aliases = {}
    if not latent_queries:
        in_specs.append(pl.BlockSpec(memory_space=pl.ANY))
        args.append(y_lat)
        aliases = {len(args) - 1: 0}
    return pl.pallas_call(
        functools.partial(_attn_kernel, lambda_init=lambda_init, n_ctx=ctx_len, with_lat=latent_queries),
        grid=(batch, h_, nq),
        in_specs=in_specs,
        out_specs=blk(tq, lambda b, h, i: (q0 + b * nq + i, h)),
        out_shape=jax.ShapeDtypeStruct((m, h_ * ATT_VAL_DIM), BF16),
        scratch_shapes=[pltpu.VMEM((n_keys, LANES), BF16), pltpu.VMEM((n_keys, 2 * ATT_VAL_DIM), BF16)],
        input_output_aliases=aliases,
        compiler_params=_cparams("parallel", "parallel", "arbitrary"),
        name="diff_attention_latent" if latent_queries else "diff_attention_context",
    )(*args)


def _rope_tables(seq):
    t = jnp.arange(seq, dtype=jnp.int32)
    row = (t // GRID_W).astype(F32)
    col = (t % GRID_W).astype(F32)
    n_pairs = ATT_HEAD_DIM // 4
    inv_freq = ROPE_BASE ** (-jnp.arange(n_pairs, dtype=F32) / n_pairs)
    ang = jnp.concatenate([row[:, None] * inv_freq, col[:, None] * inv_freq], axis=-1)
    cos = jnp.repeat(jnp.cos(ang), 2, axis=-1)
    sin = jnp.repeat(jnp.sin(ang), 2, axis=-1)
    even = (jnp.arange(ATT_HEAD_DIM) % 2 == 0)[None, :]
    both = lambda a: jnp.concatenate([a, a], axis=-1)
    return both(cos), both(jnp.where(even, -sin, 0.0)), both(jnp.where(even, 0.0, sin))


def _rwkv_prep_kernel(z_ref, zp_ref, zn_ref, smat_ref, sh_ref, w2_ref, a2_ref, g2_ref, w0_ref, a0_ref,
                      kkw_ref, ka_ref, rk_ref, seg_ref,
                      r_o, v_o, kk_o, g_o, bonus_o, lw_o, kd_o, b_o,
                      *, tile, n_lat_rows, seq, ctx_len):
    first, last = _seq_edges(pl.program_id(0), tile, n_lat_rows, seq, ctx_len)
    hd = RWKV_DIM
    zc = z_ref[...]
    shifted = _dot(smat_ref[...], zc)
    prev_row = jnp.where(first, 0.0, zp_ref[...].astype(F32)[HALO - 1:HALO, :])
    next_row = jnp.where(last, 0.0, zn_ref[...].astype(F32)[0:1, :])
    t_idx = lax.broadcasted_iota(jnp.int32, (tile, 1), 0)
    z = (sh_ref[0:1, :] * jnp.where(t_idx == 0, prev_row, shifted[:tile])
         + sh_ref[1:2, :] * zc.astype(F32)
         + sh_ref[2:3, :] * jnp.where(t_idx == tile - 1, next_row, shifted[tile:]))
    seg = seg_ref[...]
    r = z[:, 0:hd]
    k = z[:, hd:2 * hd]
    v = z[:, 2 * hd:3 * hd]
    o = 3 * hd
    wd = z[:, o:o + 2 * DECAY_LORA]
    ad = z[:, o + 2 * DECAY_LORA:o + 2 * DECAY_LORA + 2 * ICLR_LORA]
    gd = z[:, o + 2 * DECAY_LORA + 2 * ICLR_LORA:]
    kkf = k * kkw_ref[...]
    kk = kkf * lax.rsqrt(jnp.maximum(_seg_sum(kkf * kkf, seg), 1e-24))
    g = _dot(jax.nn.sigmoid(gd).astype(BF16), g2_ref[...])
    wl = w0_ref[...] + _dot(jnp.tanh(wd).astype(BF16), w2_ref[...])
    al = a0_ref[...] + _dot(ad.astype(BF16), a2_ref[...])
    softplus = jnp.maximum(-wl, 0.0) + jnp.log(1.0 + jnp.exp(-jnp.abs(wl)))
    lw = -jnp.exp(-softplus - 0.5)
    a = jax.nn.sigmoid(al)
    r_o[...] = r.astype(r_o.dtype)
    v_o[...] = v.astype(v_o.dtype)
    kk_o[...] = kk.astype(kk_o.dtype)
    g_o[...] = g.astype(g_o.dtype)
    lw_o[...] = lw
    rk = rk_ref[...]
    bonus = jnp.zeros_like(r)
    for d in range(2):
        a_d = a[:, d * hd:(d + 1) * hd]
        kd = k * (1.0 + (a_d - 1.0) * ka_ref[...])
        kd_o[:, d * hd:(d + 1) * hd] = kd.astype(kd_o.dtype)
        b_o[:, d * hd:(d + 1) * hd] = (kk * a_d).astype(b_o.dtype)
        bonus = bonus + _seg_sum(r * kd * rk, seg) * v
    bonus_o[...] = bonus.astype(bonus_o.dtype)


def _block_diag2(w):
    z = jnp.zeros_like(w[0])
    return jnp.concatenate([jnp.concatenate([w[0], z], axis=1), jnp.concatenate([z, w[1]], axis=1)], axis=0)


def _head_seg_matrix(n, head):
    idx = jnp.arange(n) // head
    return (idx[:, None] == idx[None, :]).astype(BF16)


def _rwkv_prep(zr, lp, tile, n_lat_rows, seq, ctx_len):
    m, width = zr.shape
    hd = RWKV_DIM
    full = lambda a: pl.BlockSpec(a.shape, lambda i: (0,) * a.ndim)
    shift_mat = jnp.concatenate([jnp.eye(tile, k=-1, dtype=BF16), jnp.eye(tile, k=1, dtype=BF16)], axis=0)
    consts = [shift_mat, lp['rwkv_shift'],
              _block_diag2(lp['rwkv_w2']).astype(BF16), _block_diag2(lp['rwkv_a2']).astype(BF16),
              lp['rwkv_g2'].astype(BF16),
              lp['rwkv_w0'].reshape(1, 2 * hd), lp['rwkv_a0'].reshape(1, 2 * hd),
              lp['rwkv_kk'].reshape(1, hd), lp['rwkv_ka'].reshape(1, hd), lp['rwkv_rk'].reshape(1, hd),
              _head_seg_matrix(hd, RWKV_HEAD_DIM)]
    one = jax.ShapeDtypeStruct((m, hd), BF16)
    two = jax.ShapeDtypeStruct((m, 2 * hd), BF16)
    ospec = lambda n: pl.BlockSpec((tile, n), lambda i: (i, 0))
    return pl.pallas_call(
        functools.partial(_rwkv_prep_kernel, tile=tile, n_lat_rows=n_lat_rows, seq=seq, ctx_len=ctx_len),
        grid=(m // tile,),
        in_specs=_halo_specs(tile, width, m) + [full(a) for a in consts],
        out_specs=[ospec(hd)] * 5 + [ospec(2 * hd)] * 3,
        out_shape=[one] * 5 + [jax.ShapeDtypeStruct((m, 2 * hd), F32), two, two],
        compiler_params=_cparams("parallel"),
        name="rwkv_prep",
    )(zr, zr, zr, *consts)


def _split3(x):
    h1 = x.astype(BF16)
    r1 = x - h1.astype(F32)
    h2 = r1.astype(BF16)
    h3 = (r1 - h2.astype(F32)).astype(BF16)
    return h1, h2, h3


def _scan_kernel(*refs):
    ins = (refs[0:6], refs[6:12])
    y_refs = refs[12:14]
    s_ref = refs[14]
    c_len = SCAN_CHUNK
    n_sub = refs[0].shape[0] // c_len
    n_pairs = RWKV_DIM // LANES

    @pl.when(pl.program_id(1) == 0)
    def _():
        s_ref[...] = jnp.zeros_like(s_ref)

    row = lax.broadcasted_iota(jnp.int32, (c_len, c_len), 0)
    col = lax.broadcasted_iota(jnp.int32, (c_len, c_len), 1)
    row_c = lax.broadcasted_iota(jnp.int32, (c_len, LANES), 0)
    col_c = lax.broadcasted_iota(jnp.int32, (c_len, LANES), 1) & (c_len - 1)
    row_4 = lax.broadcasted_iota(jnp.int32, (c_len, 2 * LANES), 0)
    col_4 = lax.broadcasted_iota(jnp.int32, (c_len, 2 * LANES), 1) & (c_len - 1)
    eye_c = jnp.where(row_c == col_c, 1.0, 0.0)
    head0 = lax.broadcasted_iota(jnp.int32, (c_len, LANES), 1) < RWKV_HEAD_DIM
    srow = lax.broadcasted_iota(jnp.int32, (LANES, LANES), 0) // RWKV_HEAD_DIM
    scol = lax.broadcasted_iota(jnp.int32, (LANES, LANES), 1) // RWKV_HEAD_DIM
    same_head = srow == scol

    def by_head(x):
        zero = jnp.zeros_like(x)
        return jnp.concatenate([jnp.where(head0, x, zero), jnp.where(head0, zero, x)], axis=0)

    masks = []
    for d in range(2):
        ahead = (row - col) if d == 0 else (col - row)
        ahead_c = (row_c - col_c) if d == 0 else (col_c - row_c)
        ahead_4 = (row_4 - col_4) if d == 0 else (col_4 - row_4)
        tri = jnp.where(ahead >= 0, 1.0, 0.0).astype(BF16)
        pair_masks = []
        for sh in range(int(math.log2(c_len))):
            same_pair = (row_c >> (sh + 1)) == (col_c >> (sh + 1))
            pair_masks.append(same_pair & ((row_c >> sh) != (col_c >> sh)) & (ahead_c > 0))
        masks.append(dict(pair=pair_masks, strict=ahead_c > 0, incl4=ahead_4 >= 0, tri=tri))

    def scaled_operands(d, g):
        r_ref, lw_ref, k_ref, v_ref, kk_ref, b_ref = ins[d]
        rows = pl.ds(g * c_len, c_len)
        tri = masks[d]['tri']
        lw = lw_ref[rows, :]
        h1, h2, h3 = _split3(lw)
        c = _dot(tri, h1) + _dot(tri, h2) + _dot(tri, h3)
        tot = jnp.sum(lw, axis=0, keepdims=True)
        e_neg = jnp.exp(-c)
        dec = jnp.exp(tot - c)
        f32 = lambda ref: ref[rows, :].astype(F32)
        b_f, k_f = f32(b_ref), f32(k_ref)
        return dict(
            a_t=-f32(kk_ref) * jnp.exp(c - lw), r_t=f32(r_ref) * jnp.exp(c),
            b_t=(b_f * e_neg).astype(BF16), k_t=(k_f * e_neg).astype(BF16),
            b_p=(b_f * dec).astype(BF16), k_p=(k_f * dec).astype(BF16),
            p_tot=jnp.exp(tot), v=v_ref[rows, :])

    lanes_of = lambda p: slice(p * LANES, (p + 1) * LANES)
    sub_of = lambda d, step: step if d == 0 else n_sub - 1 - step
    state = {(d, p): s_ref[d, p] for d in range(2) for p in range(n_pairs)}
    wave_len = min(SCAN_WAVE, n_sub)

    n_waves = n_sub // wave_len
    wave_subs = lambda wave: [(d, sub_of(d, s)) for s in range(wave * wave_len, (wave + 1) * wave_len)
                              for d in range(2)]
    next_ops = {dg: scaled_operands(*dg) for dg in wave_subs(0)}
    for wave in range(n_waves):
        steps = range(wave * wave_len, (wave + 1) * wave_len)
        ops, next_ops = next_ops, {}
        pending = wave_subs(wave + 1) if wave + 1 < n_waves else []
        groups = [(d, g, p) for (d, g) in ops for p in range(n_pairs)]

        scores, v_bd = {}, {}
        for gr in groups:
            d, g, p = gr
            sl = lanes_of(p)
            lhs = jnp.concatenate([ops[d, g]['a_t'][:, sl], ops[d, g]['r_t'][:, sl]], axis=0).astype(BF16)
            rhs = jnp.concatenate([by_head(ops[d, g]['b_t'][:, sl]), by_head(ops[d, g]['k_t'][:, sl])], axis=0)
            scores[gr] = _dot_nt(lhs, rhs)
            v_bd[gr] = by_head(ops[d, g]['v'][:, sl])
        w0, l_mat, x_inv = {}, {}, {}
        for gr in groups:
            d = gr[0]
            top = scores[gr][:c_len]
            l_mat[gr] = top[:, :LANES]
            w0[gr] = _dot(jnp.where(masks[d]['strict'], top[:, LANES:], 0.0).astype(BF16), v_bd[gr])
            x_inv[gr] = eye_c + jnp.where(masks[d]['pair'][0], l_mat[gr], 0.0)

        for level in range(1, int(math.log2(c_len))):
            t1 = {}
            for gr in groups:
                pm = masks[gr[0]]['pair'][level]
                t1[gr] = _dot(jnp.where(pm, l_mat[gr], 0.0).astype(BF16), by_head(x_inv[gr].astype(BF16)))
            for gr in groups:
                x_inv[gr] = x_inv[gr] + _dot(x_inv[gr].astype(BF16), by_head(t1[gr].astype(BF16)))
            if pending:
                dg = pending.pop(0)
                next_ops[dg] = scaled_operands(*dg)
        for dg in pending:
            next_ops[dg] = scaled_operands(*dg)
        xa_u0 = {}
        for gr in groups:
            d, g, p = gr
            rhs = jnp.concatenate([by_head(ops[d, g]['a_t'][:, lanes_of(p)].astype(BF16)),
                                   by_head(w0[gr].astype(BF16))], axis=1)
            xa_u0[gr] = _dot(x_inv[gr].astype(BF16), rhs)
        p_row_of = {dg: i for i, dg in enumerate(ops)}
        p_all = jnp.concatenate([ops[dg]['p_tot'] for dg in ops]
                                + [jnp.zeros((LANES - len(ops), RWKV_DIM), F32)], axis=0)
        p_t = [jnp.transpose(p_all[:, lanes_of(p)]) for p in range(n_pairs)]
        bk_t = {}
        for d, g, p in groups:
            sl = lanes_of(p)
            bk_t[d, g, p] = jnp.transpose(
                jnp.concatenate([ops[d, g]['b_p'][:, sl], ops[d, g]['k_p'][:, sl]], axis=0))
        lhs_t, p_col = {}, {}
        for gr in groups:
            d, g, p = gr
            lhs_t[gr] = jnp.concatenate([xa_u0[gr][:, :LANES].astype(BF16),
                                         ops[d, g]['r_t'][:, lanes_of(p)].astype(BF16)], axis=0)
            r_idx = p_row_of[d, g]
            p_col[gr] = jnp.broadcast_to(p_t[p][:, r_idx:r_idx + 1], (LANES, LANES))

        for step in steps:
            live = [(d, sub_of(d, step), p) for d in range(2) for p in range(n_pairs)]
            prod = {gr: _dot(lhs_t[gr], state[gr[0], gr[2]].astype(BF16)) for gr in live}
            u = {gr: (prod[gr][:c_len] + xa_u0[gr][:, LANES:]).astype(BF16) for gr in live}
            upd = {gr: _dot(bk_t[gr],
                            jnp.concatenate([u[gr], ops[gr[0], gr[1]]['v'][:, lanes_of(gr[2])]], axis=0))
                   for gr in live}
            for gr in live:
                d, g, p = gr
                state[d, p] = p_col[gr] * state[d, p] + jnp.where(same_head, upd[gr], 0.0)
            for gr in live:
                d, g, p = gr
                y = _dot(jnp.where(masks[d]['incl4'], scores[gr][c_len:], 0.0).astype(BF16),
                         jnp.concatenate([by_head(u[gr]), v_bd[gr]], axis=0))
                y_refs[d][pl.ds(g * c_len, c_len), lanes_of(p)] = prod[gr][c_len:] + y
    for (d, p), s_new in state.items():
        s_ref[d, p] = s_new


def _rwkv_scan(r, lw, kd, v, kk, bvec, batch, seq, ctx_len):
    m, hd = r.shape
    n_sub = _pick_tile(SCAN_SUBCHUNKS, ctx_len // SCAN_CHUNK, seq // SCAN_CHUNK)
    c_len = n_sub * SCAN_CHUNK
    n_ctx, n_lat = ctx_len // c_len, seq // c_len
    ctx0 = batch * seq // c_len

    def rowblk(d):
        def fn(b, j):
            jc = j if d == 0 else n_ctx - 1 - j
            jl = (j - n_ctx) if d == 0 else n_lat - 1 - (j - n_ctx)
            return jnp.where(j < n_ctx, ctx0 + b * n_ctx + jc, b * n_lat + jl)
        return fn

    in_specs, args = [], []
    for d in range(2):
        shared = pl.BlockSpec((c_len, hd), lambda b, j, f=rowblk(d): (f(b, j), 0))
        per_dir = pl.BlockSpec((c_len, hd), lambda b, j, f=rowblk(d), d=d: (f(b, j), d))
        in_specs += [shared, per_dir, per_dir, shared, shared, per_dir]
        args += [r, lw, kd, v, kk, bvec]
    out = jax.ShapeDtypeStruct((m, hd), F32)
    return pl.pallas_call(
        _scan_kernel,
        grid=(batch, n_ctx + n_lat),
        in_specs=in_specs,
        out_specs=[pl.BlockSpec((c_len, hd), lambda b, j, f=rowblk(d): (f(b, j), 0)) for d in range(2)],
        out_shape=[out, out],
        scratch_shapes=[pltpu.VMEM((2, hd // LANES, LANES, LANES), F32)],
        compiler_params=_cparams("parallel", "arbitrary"),
        name="rwkv_scan",
    )(*args)


def _merge_kernel(x_ref, yc_ref, ya_ref, yf_ref, yb_ref, bonus_ref, rg_ref, gg_ref, gb_ref, seg_ref,
                  gt_ref, pc_ref, pa_ref, pr_ref, wo_ref, gate_ref, o_ref):
    d = x_ref.shape[1]
    seg = seg_ref[...]
    ys = yf_ref[...] + yb_ref[...]
    inv_n = 1.0 / RWKV_HEAD_DIM
    mu = _seg_sum(ys, seg) * inv_n
    dlt = ys - mu
    var = _seg_sum(dlt * dlt, seg) * inv_n
    yn = dlt * lax.rsqrt(var + GN_EPS) * gg_ref[...] + gb_ref[...]
    y_rw = ((yn + bonus_ref[...].astype(F32)) * rg_ref[...].astype(F32)).astype(BF16)
    g = jax.nn.sigmoid(gt_ref[...].astype(F32))
    mix = (g[:, 0:d] * _dot(yc_ref[...], pc_ref[...])
           + g[:, d:2 * d] * _dot(ya_ref[...], pa_ref[...])
           + g[:, 2 * d:] * _dot(y_rw, pr_ref[...]))
    o_ref[...] = x_ref[...] + gate_ref[...] * _dot(mix.astype(BF16), wo_ref[...])


def _merge(x, rows, yc, ya, yf, yb, bonus, rg, gn_g, gn_b, gates, pc, pa, pr, wo, mod_tab, tm, row_of_tile):
    d = x.shape[1]
    hd = yf.shape[1]
    seg = _head_seg_matrix(hd, RWKV_HEAD_DIM)
    gn_g, gn_b = gn_g.reshape(1, hd), gn_b.reshape(1, hd)
    tok = lambda a: pl.BlockSpec((tm, a.shape[1]), lambda i: (i, 0))
    full = lambda a: pl.BlockSpec(a.shape, lambda i: (0, 0))
    return pl.pallas_call(
        _merge_kernel,
        grid=(rows // tm,),
        in_specs=[tok(x), tok(yc), tok(ya), tok(yf), tok(yb), tok(bonus), tok(rg), full(gn_g), full(gn_b),
                  full(seg), tok(gates), full(pc), full(pa), full(pr), full(wo),
                  pl.BlockSpec((None, None, 1, d), lambda i: (row_of_tile(i), 2, 0, 0))],
        out_specs=tok(x),
        out_shape=jax.ShapeDtypeStruct((rows, d), F32),
        compiler_params=_cparams("parallel"),
        name="merge_out_proj",
    )(x, yc, ya, yf, yb, bonus, rg, gn_g, gn_b, seg, gates, pc, pa, pr, wo, mod_tab)


def _mlp_kernel(*refs, final):
    if final:
        x_ref, g_ref, sh_ref, sc_ref, gate_ref, w1_ref, w2_ref, fg_ref, o_ref, h_ref, acc_ref = refs
    else:
        x_ref, g_ref, sh_ref, sc_ref, gate_ref, w1_ref, w2_ref, o_ref, h_ref, acc_ref = refs
    f = pl.program_id(1)

    @pl.when(f == 0)
    def _():
        h_ref[...] = _modulated_norm(x_ref[...], g_ref[...], sh_ref[...], sc_ref[...]).astype(BF16)
        acc_ref[...] = jnp.zeros_like(acc_ref)

    t = jnp.maximum(_dot(h_ref[...], w1_ref[...]), 0.0)
    acc_ref[...] += _dot((t * t).astype(BF16), w2_ref[...])

    @pl.when(f == pl.num_programs(1) - 1)
    def _():
        y = x_ref[...] + gate_ref[...] * acc_ref[...]
        if final:
            y = y * lax.rsqrt(jnp.mean(y * y, axis=-1, keepdims=True) + NORM_EPS) * fg_ref[...]
        o_ref[...] = y


def _mlp(x, rows, gain, mod_tab, w1, w2, final_g, tm, tf, row_of_tile):
    d = x.shape[1]
    dff = w1.shape[1]
    final = final_g is not None
    tab = lambda which: pl.BlockSpec((None, None, 1, d), lambda i, f: (row_of_tile(i), which, 0, 0))
    in_specs = [pl.BlockSpec((tm, d), lambda i, f: (i, 0)),
                pl.BlockSpec((1, d), lambda i, f: (0, 0)),
                tab(3), tab(4), tab(5),
                pl.BlockSpec((d, tf), lambda i, f: (0, f)),
                pl.BlockSpec((tf, d), lambda i, f: (f, 0))]
    args = [x, gain.reshape(1, d), mod_tab, mod_tab, mod_tab, w1, w2]
    if final:
        in_specs.append(pl.BlockSpec((1, d), lambda i, f: (0, 0)))
        args.append(final_g.reshape(1, d))
    return pl.pallas_call(
        functools.partial(_mlp_kernel, final=final),
        grid=(rows // tm, dff // tf),
        in_specs=in_specs,
        out_specs=pl.BlockSpec((tm, d), lambda i, f: (i, 0)),
        out_shape=jax.ShapeDtypeStruct((rows, d), F32),
        scratch_shapes=[pltpu.VMEM((tm, d), BF16), pltpu.VMEM((tm, d), F32)],
        compiler_params=_cparams("parallel", "arbitrary"),
        name="mlp_final" if final else "mlp",
    )(*args)


def kernel(x, c, ctx, c_ctx, mod_w, mod_b, norm1_g, norm2_g, w_in, conv_dw_w, conv_dw_b, conv_ln_g, conv_ln_b, p_conv, att_lq1, att_lk1, att_lq2, att_lk2, att_subln_g, p_att, rwkv_shift, rwkv_w0, rwkv_w2, rwkv_a0, rwkv_a2, rwkv_g2, rwkv_kk, rwkv_ka, rwkv_rk, rwkv_gn_g, rwkv_gn_b, p_rwkv, w_out, mlp_w1, mlp_w2, final_g):
    batch, seq, d = x.shape
    ctx_len = ctx.shape[1]
    depth = mod_w.shape[0]
    n_lat = batch * seq
    conv_w = 2 * conv_dw_w.shape[2]
    att_w = 3 * ATT_HEADS * ATT_VAL_DIM
    rw_w = 3 * RWKV_DIM + 2 * DECAY_LORA + 2 * ICLR_LORA + GATE_LORA
    assert w_in.shape[2] == conv_w + att_w + rw_w + 3 * d
    assert seq % SCAN_CHUNK == 0 and ctx_len % SCAN_CHUNK == 0 and (batch * seq) % ctx_len == 0

    tm = _pick_tile(1024, seq, batch * ctx_len)
    tmm = _pick_tile(512, seq, batch * ctx_len)
    tc = _pick_tile(256, seq, ctx_len)
    tiles_per_seq = seq // tm
    row_of_tile = lambda i: jnp.minimum(i // tiles_per_seq, batch)
    row_of_tile_m = lambda i: jnp.minimum(i // (seq // tmm), batch)

    xs = jnp.concatenate([x.reshape(n_lat, d), ctx.reshape(batch * ctx_len, d)], axis=0)
    pad = (-(batch + 1)) % 8
    c_all = jnp.concatenate([c, c_ctx[None, :], jnp.zeros((pad, d), F32)], axis=0)
    rope_tabs = _rope_tables(seq)

    for l in range(depth):
        last = l == depth - 1
        lambda_init = 0.8 - 0.6 * math.exp(-0.3 * l)
        mod = _modulation(c_all, mod_w[l], mod_b[l])
        mod_tab = mod[:batch + 1].reshape(batch + 1, 6, 1, d)

        wl = w_in[l].astype(BF16)
        o1, o2, o3 = conv_w, conv_w + att_w, conv_w + att_w + rw_w
        proj = functools.partial(_project, xs, norm1_g[l], mod_tab, 0, tm=tm, row_of_tile=row_of_tile)
        zc = proj(wl[:, :o1], BF16, tn=o1, name="proj_conv")
        qkv = proj(wl[:, o1:o2], BF16, tn=att_w, name="proj_qkv")
        zr = proj(wl[:, o2:o3], BF16, tn=rw_w, name="proj_rwkv")
        gates = proj(wl[:, o3:], BF16, tn=3 * d, name="proj_gates")

        y_conv = _conv_branch(zc, conv_dw_w[l], conv_dw_b[l], conv_ln_g[l], conv_ln_b[l],
                              tc, n_lat, seq, ctx_len)

        lam_vecs = jnp.stack([att_lq1[l], att_lk1[l], att_lq2[l], att_lk2[l]])
        y_att = _attention(qkv, rope_tabs, lam_vecs, att_subln_g[l], lambda_init, batch, seq, ctx_len)
        if not last:
            y_att = _attention(qkv, None, lam_vecs, att_subln_g[l], lambda_init, batch, seq, ctx_len, y_att)

        lp = dict(rwkv_shift=rwkv_shift[l], rwkv_w0=rwkv_w0[l], rwkv_w2=rwkv_w2[l], rwkv_a0=rwkv_a0[l],
                  rwkv_a2=rwkv_a2[l], rwkv_g2=rwkv_g2[l], rwkv_kk=rwkv_kk[l], rwkv_ka=rwkv_ka[l],
                  rwkv_rk=rwkv_rk[l])
        r, v, kk, g, bonus, lw, kd, bvec = _rwkv_prep(zr, lp, tc, n_lat, seq, ctx_len)
        yf, yb = _rwkv_scan(r, lw, kd, v, kk, bvec, batch, seq, ctx_len)

        rows = n_lat if last else xs.shape[0]
        xs = _merge(xs, rows, y_conv, y_att, yf, yb, bonus, g, rwkv_gn_g[l], rwkv_gn_b[l], gates,
                    p_conv[l].astype(BF16), p_att[l].astype(BF16),
                    p_rwkv[l].astype(BF16), w_out[l].astype(BF16), mod_tab, tmm, row_of_tile_m)
        xs = _mlp(xs, rows, norm2_g[l], mod_tab, mlp_w1[l].astype(BF16), mlp_w2[l].astype(BF16),
                  final_g if last else None, tm, _pick_tile(512, mlp_w1.shape[2]), row_of_tile)
    return xs.reshape(batch, seq, d)
```

```python
import functools
import math

import jax
import jax.numpy as jnp
from jax import lax
from jax.experimental import pallas as pl
from jax.experimental.pallas import tpu as pltpu

F32 = jnp.float32
BF16 = jnp.bfloat16

GRID_W = 64
NORM_EPS = 1e-6
CONV_WIDTH = 31
LN_EPS = 1e-5
ATT_HEADS = 4
ATT_HEAD_DIM = 64
ATT_VAL_DIM = 2 * ATT_HEAD_DIM
ROPE_BASE = 10000.0
SUBLN_EPS = 1e-5
RWKV_HEADS = 8
RWKV_HEAD_DIM = 64
RWKV_DIM = RWKV_HEADS * RWKV_HEAD_DIM
DECAY_LORA = 64
ICLR_LORA = 64
GATE_LORA = 128
GN_EPS = 64e-5

LANES = 128
SUBLANES = 8
HALO = 16
SCAN_CHUNK = 64
SCAN_SUBCHUNKS = 4
SCAN_WAVE = 2
MLP_FF_TILE = 1024
PROJ_ROW_PARTS = 2
ATT_Q_TILE = 1024
ATT_ROW_PARTS = 8
VMEM_LIMIT = 56 << 20


def _cparams(*sem):
    return pltpu.CompilerParams(dimension_semantics=sem, vmem_limit_bytes=VMEM_LIMIT)


def _pick_tile(limit, *sizes):
    t = limit
    while any(s % t for s in sizes):
        t //= 2
    return t


def _dot(a, b):
    return jnp.dot(a, b, preferred_element_type=F32)


def _dot_nt(a, b):
    return lax.dot_general(a, b, (((1,), (1,)), ((), ())), preferred_element_type=F32)


def _dot_tn(a, b):
    return lax.dot_general(a, b, (((0,), (0,)), ((), ())), preferred_element_type=F32)


def _seg_sum(x, seg):
    hi = x.astype(BF16)
    lo = (x - hi.astype(F32)).astype(BF16)
    return _dot(hi, seg) + _dot(lo, seg)


def _modulated_norm(x, gain, shift, scale):
    y = x * lax.rsqrt(jnp.mean(x * x, axis=-1, keepdims=True) + NORM_EPS)
    return (y * gain) * (1.0 + scale) + shift


def _mod_kernel(c_ref, w_ref, b_ref, o_ref):
    c = c_ref[...]
    s = c * jax.nn.sigmoid(c)
    o_ref[...] = _dot(s.astype(BF16), w_ref[...].astype(BF16)) + b_ref[...]


def _modulation(c_all, mod_w, mod_b):
    rows, d = c_all.shape
    n = mod_w.shape[1]
    tn = _pick_tile(1024, n // 6)
    return pl.pallas_call(
        _mod_kernel,
        grid=(n // tn,),
        in_specs=[pl.BlockSpec((rows, d), lambda j: (0, 0)),
                  pl.BlockSpec((d, tn), lambda j: (0, j)),
                  pl.BlockSpec((1, tn), lambda j: (0, j))],
        out_specs=pl.BlockSpec((rows, tn), lambda j: (0, j)),
        out_shape=jax.ShapeDtypeStruct((rows, n), F32),
        compiler_params=_cparams("arbitrary"),
        name="adaln_modulation",
    )(c_all, mod_w, mod_b.reshape(1, n))


def _norm_proj_kernel(x_ref, g_ref, sh_ref, sc_ref, w_ref, o_ref, h_ref):
    part = x_ref.shape[0] // PROJ_ROW_PARTS
    for i in range(PROJ_ROW_PARTS):
        rows = pl.ds(i * part, part)
        h = _modulated_norm(x_ref[rows, :], g_ref[...], sh_ref[...], sc_ref[...]).astype(BF16)
        h_ref[rows, :] = h
        o_ref[rows, :] = _dot(h, w_ref[...]).astype(o_ref.dtype)


def _norm_project(x, gain, mod_tab, which_shift, w, tm, row_of_tile, name):
    m, d = x.shape
    n = w.shape[1]
    tab = lambda which: pl.BlockSpec((None, None, 1, d), lambda i: (row_of_tile(i), which, 0, 0))
    return pl.pallas_call(
        _norm_proj_kernel,
        grid=(m // tm,),
        in_specs=[pl.BlockSpec((tm, d), lambda i: (i, 0)),
                  pl.BlockSpec((1, d), lambda i: (0, 0)),
                  tab(which_shift), tab(which_shift + 1),
                  pl.BlockSpec((d, n), lambda i: (0, 0))],
        out_specs=[pl.BlockSpec((tm, n), lambda i: (i, 0)), pl.BlockSpec((tm, d), lambda i: (i, 0))],
        out_shape=[jax.ShapeDtypeStruct((m, n), BF16), jax.ShapeDtypeStruct((m, d), BF16)],
        compiler_params=_cparams("parallel"),
        name=name,
    )(x, gain.reshape(1, d), mod_tab, mod_tab, w)


def _proj_kernel(h_ref, w_ref, o_ref):
    o_ref[...] = _dot(h_ref[...], w_ref[...]).astype(o_ref.dtype)


def _project(h, w, tm, name):
    m, d = h.shape
    n = w.shape[1]
    return pl.pallas_call(
        _proj_kernel,
        grid=(m // tm,),
        in_specs=[pl.BlockSpec((tm, d), lambda i: (i, 0)), pl.BlockSpec((d, n), lambda i: (0, 0))],
        out_specs=pl.BlockSpec((tm, n), lambda i: (i, 0)),
        out_shape=jax.ShapeDtypeStruct((m, n), BF16),
        compiler_params=_cparams("parallel"),
        name=name,
    )(h, w)


def _seq_edges(i, tile, n_lat_rows, seq, ctx_len):
    n_lat_tiles = n_lat_rows // tile
    is_lat = i < n_lat_tiles
    per = jnp.where(is_lat, seq // tile, ctx_len // tile)
    pos = jnp.where(is_lat, i, i - n_lat_tiles) % per
    return pos == 0, pos == per - 1


def _conv_kernel(z_ref, zp_ref, zn_ref, w_ref, b_ref, lg_ref, lb_ref, o_ref, ext_ref, rot_ref,
                 *, tile, n_lat_rows, seq, ctx_len):
    first, last = _seq_edges(pl.program_id(0), tile, n_lat_rows, seq, ctx_len)
    cd = o_ref.shape[1]

    def glu(z):
        z = z.astype(F32)
        return z[:, :cd] * jax.nn.sigmoid(z[:, cd:])

    ext_ref[0:HALO, :] = jnp.where(first, 0.0, glu(zp_ref[...]))
    ext_ref[HALO:HALO + tile, :] = glu(z_ref[...])
    ext_ref[HALO + tile:, :] = jnp.where(last, 0.0, glu(zn_ref[...]))
    span = rot_ref.shape[1]
    for r in range(SUBLANES):
        rot_ref[r] = ext_ref[pl.ds(r, span), :]
    half = CONV_WIDTH // 2
    acc = jnp.zeros((tile, cd), F32)
    for k in range(CONV_WIDTH):
        start = HALO - half + k
        acc = acc + w_ref[k:k + 1, :] * rot_ref[start % SUBLANES, pl.ds(start - start % SUBLANES, tile), :]
    u = acc + b_ref[...]
    mu = jnp.mean(u, axis=-1, keepdims=True)
    dlt = u - mu
    var = jnp.mean(dlt * dlt, axis=-1, keepdims=True)
    y = dlt * lax.rsqrt(var + LN_EPS) * lg_ref[...] + lb_ref[...]
    o_ref[...] = (y * jax.nn.sigmoid(y)).astype(o_ref.dtype)


def _halo_specs(tile, width, m):
    per = tile // HALO
    last_blk = m // HALO - 1
    return [pl.BlockSpec((tile, width), lambda i: (i, 0)),
            pl.BlockSpec((HALO, width), lambda i: (jnp.maximum(i * per - 1, 0), 0)),
            pl.BlockSpec((HALO, width), lambda i: (jnp.minimum((i + 1) * per, last_blk), 0))]


def _conv_branch(zc, dw_w, dw_b, ln_g, ln_b, tile, n_lat_rows, seq, ctx_len):
    m, width = zc.shape
    cd = width // 2
    vec = pl.BlockSpec((1, cd), lambda i: (0, 0))
    return pl.pallas_call(
        functools.partial(_conv_kernel, tile=tile, n_lat_rows=n_lat_rows, seq=seq, ctx_len=ctx_len),
        grid=(m // tile,),
        in_specs=_halo_specs(tile, width, m) + [pl.BlockSpec((CONV_WIDTH, cd), lambda i: (0, 0)),
                                                vec, vec, vec],
        out_specs=pl.BlockSpec((tile, cd), lambda i: (i, 0)),
        out_shape=jax.ShapeDtypeStruct((m, cd), BF16),
        scratch_shapes=[pltpu.VMEM((tile + 2 * HALO, cd), F32),
                        pltpu.VMEM((SUBLANES, tile + 2 * HALO - SUBLANES, cd), F32)],
        compiler_params=_cparams("parallel"),
        name="conv_branch",
    )(zc, zc, zc, dw_w, dw_b.reshape(1, cd), ln_g.reshape(1, cd), ln_b.reshape(1, cd))


def _rope(x, cos, sin_next, sin_prev):
    return x * cos + pltpu.roll(x, LANES - 1, 1) * sin_next + pltpu.roll(x, 1, 1) * sin_prev


def _attn_kernel(*refs, lambda_init, n_ctx, with_lat):
    if with_lat:
        (q_ref, kc_ref, vc_ref, kl_ref, vl_ref, cq_ref, snq_ref, spq_ref, ck_ref, snk_ref, spk_ref,
         lam_ref, g_ref, o_ref, ks_ref, vs_ref) = refs
    else:
        q_ref, kc_ref, vc_ref, lam_ref, g_ref, _, o_ref, ks_ref, vs_ref = refs
    dv = ATT_VAL_DIM

    @pl.when(pl.program_id(2) == 0)
    def _():
        vs_ref[:, dv:] = jnp.ones((vs_ref.shape[0], dv), BF16)
        ks_ref[0:n_ctx, :] = kc_ref[...]
        vs_ref[0:n_ctx, 0:dv] = vc_ref[...]
        if with_lat:
            k = _rope(kl_ref[...].astype(F32), ck_ref[...], snk_ref[...], spk_ref[...])
            ks_ref[n_ctx:, :] = k.astype(BF16)
            vs_ref[n_ctx:, 0:dv] = vl_ref[...]

    tq = q_ref.shape[0]
    q = q_ref[...].astype(F32)
    if with_lat:
        q = _rope(q, cq_ref[...], snq_ref[...], spq_ref[...])
    q = q * (math.log2(math.e) / math.sqrt(ATT_HEAD_DIM))
    first_map = lax.broadcasted_iota(jnp.int32, q.shape, 1) < ATT_HEAD_DIM
    qs = jnp.concatenate([jnp.where(first_map, q, 0.0), jnp.where(first_map, 0.0, q)], axis=0).astype(BF16)
    part = (2 * tq) // ATT_ROW_PARTS
    scores = lambda i: _dot_nt(qs[i * part:(i + 1) * part], ks_ref[...])
    s_next = scores(0)
    outs = []
    for i in range(ATT_ROW_PARTS):
        s = s_next
        if i + 1 < ATT_ROW_PARTS:
            s_next = scores(i + 1)
        p = jnp.exp2((s - jnp.max(s, axis=-1, keepdims=True)).astype(BF16))
        outs.append(_dot(p, vs_ref[...]))
    ol = jnp.concatenate(outs, axis=0)
    o = ol[:, 0:dv] / ol[:, dv:]
    lq = lam_ref[...]
    lam = (jnp.exp(jnp.sum(lq[0:1] * lq[1:2], axis=-1, keepdims=True))
           - jnp.exp(jnp.sum(lq[2:3] * lq[3:4], axis=-1, keepdims=True)) + lambda_init)
    a = o[:tq] - lam * o[tq:]
    y = a * lax.rsqrt(jnp.mean(a * a, axis=-1, keepdims=True) + SUBLN_EPS) * g_ref[...]
    o_ref[...] = (y * (1.0 - lambda_init)).astype(o_ref.dtype)


def _attention(qkv, rope_tabs, lam_vecs, subln_g, lambda_init, batch, seq, ctx_len, y_lat=None):
    m = qkv.shape[0]
    latent_queries = y_lat is None
    h_ = ATT_HEADS
    ctx_blk0 = batch * seq // ctx_len
    tq = _pick_tile(ATT_Q_TILE, seq if latent_queries else ctx_len)
    nq = (seq if latent_queries else ctx_len) // tq
    q0 = 0 if latent_queries else batch * seq // tq
    blk = lambda rows, fn: pl.BlockSpec((rows, LANES), fn)
    in_specs = [blk(tq, lambda b, h, i: (q0 + b * nq + i, h)),
                blk(ctx_len, lambda b, h, i: (ctx_blk0 + b, h_ + h)),
                blk(ctx_len, lambda b, h, i: (ctx_blk0 + b, 2 * h_ + h))]
    args = [qkv, qkv, qkv]
    n_keys = ctx_len
    if latent_queries:
        n_keys += seq
        in_specs += [blk(seq, lambda b, h, i: (b, h_ + h)), blk(seq, lambda b, h, i: (b, 2 * h_ + h))]
        in_specs += [blk(tq, lambda b, h, i: (i, 0))] * 3 + [blk(seq, lambda b, h, i: (0, 0))] * 3
        args += [qkv, qkv] + list(rope_tabs) + list(rope_tabs)
    in_specs += [pl.BlockSpec((4, ATT_HEAD_DIM), lambda b, h, i: (0, 0)),
                 pl.BlockSpec((1, ATT_VAL_DIM), lambda b, h, i: (0, 0))]
    args += [lam_vecs, subln_g.reshape(1, ATT_VAL_DIM)]
    aliases = {}
    if not latent_queries:
        in_specs.append(pl.BlockSpec(memory_space=pl.ANY))
        args.append(y_lat)
        aliases = {len(args) - 1: 0}
    return pl.pallas_call(
        functools.partial(_attn_kernel, lambda_init=lambda_init, n_ctx=ctx_len, with_lat=latent_queries),
        grid=(batch, h_, nq),
        in_specs=in_specs,
        out_specs=blk(tq, lambda b, h, i: (q0 + b * nq + i, h)),
        out_shape=jax.ShapeDtypeStruct((m, h_ * ATT_VAL_DIM), BF16),
        scratch_shapes=[pltpu.VMEM((n_keys, LANES), BF16), pltpu.VMEM((n_keys, 2 * ATT_VAL_DIM), BF16)],
        input_output_aliases=aliases,
        compiler_params=_cparams("parallel", "parallel", "arbitrary"),
        name="diff_attention_latent" if latent_queries else "diff_attention_context",
    )(*args)


def _rope_tables(seq):
    t = jnp.arange(seq, dtype=jnp.int32)
    row = (t // GRID_W).astype(F32)
    col = (t % GRID_W).astype(F32)
    n_pairs = ATT_HEAD_DIM // 4
    inv_freq = ROPE_BASE ** (-jnp.arange(n_pairs, dtype=F32) / n_pairs)
    ang = jnp.concatenate([row[:, None] * inv_freq, col[:, None] * inv_freq], axis=-1)
    cos = jnp.repeat(jnp.cos(ang), 2, axis=-1)
    sin = jnp.repeat(jnp.sin(ang), 2, axis=-1)
    even = (jnp.arange(ATT_HEAD_DIM) % 2 == 0)[None, :]
    both = lambda a: jnp.concatenate([a, a], axis=-1)
    return both(cos), both(jnp.where(even, -sin, 0.0)), both(jnp.where(even, 0.0, sin))


def _rwkv_prep_kernel(z_ref, zp_ref, zn_ref, smat_ref, sh_ref, w2_ref, a2_ref, g2_ref, w0_ref, a0_ref,
                      kkw_ref, ka_ref, rk_ref, seg_ref,
                      r_o, v_o, kk_o, g_o, bonus_o, lw_o, kd_o, b_o,
                      *, tile, n_lat_rows, seq, ctx_len):
    first, last = _seq_edges(pl.program_id(0), tile, n_lat_rows, seq, ctx_len)
    hd = RWKV_DIM
    zc = z_ref[...]
    shifted = _dot(smat_ref[...], zc)
    prev_row = jnp.where(first, 0.0, zp_ref[...].astype(F32)[HALO - 1:HALO, :])
    next_row = jnp.where(last, 0.0, zn_ref[...].astype(F32)[0:1, :])
    t_idx = lax.broadcasted_iota(jnp.int32, (tile, 1), 0)
    z = (sh_ref[0:1, :] * jnp.where(t_idx == 0, prev_row, shifted[:tile])
         + sh_ref[1:2, :] * zc.astype(F32)
         + sh_ref[2:3, :] * jnp.where(t_idx == tile - 1, next_row, shifted[tile:]))
    seg = seg_ref[...]
    r = z[:, 0:hd]
    k = z[:, hd:2 * hd]
    v = z[:, 2 * hd:3 * hd]
    o = 3 * hd
    wd = z[:, o:o + 2 * DECAY_LORA]
    ad = z[:, o + 2 * DECAY_LORA:o + 2 * DECAY_LORA + 2 * ICLR_LORA]
    gd = z[:, o + 2 * DECAY_LORA + 2 * ICLR_LORA:]
    kkf = k * kkw_ref[...]
    kk = kkf * lax.rsqrt(jnp.maximum(_seg_sum(kkf * kkf, seg), 1e-24))
    g = _dot(jax.nn.sigmoid(gd).astype(BF16), g2_ref[...])
    wl = w0_ref[...] + _dot(jnp.tanh(wd).astype(BF16), w2_ref[...])
    al = a0_ref[...] + _dot(ad.astype(BF16), a2_ref[...])
    softplus = jnp.maximum(-wl, 0.0) + jnp.log(1.0 + jnp.exp(-jnp.abs(wl)))
    lw = -jnp.exp(-softplus - 0.5)
    a = jax.nn.sigmoid(al)
    r_o[...] = r.astype(r_o.dtype)
    v_o[...] = v.astype(v_o.dtype)
    kk_o[...] = kk.astype(kk_o.dtype)
    g_o[...] = g.astype(g_o.dtype)
    lw_o[...] = lw
    kd_sum = jnp.zeros_like(r)
    for d in range(2):
        a_d = a[:, d * hd:(d + 1) * hd]
        kd = k * (1.0 + (a_d - 1.0) * ka_ref[...])
        kd_o[:, d * hd:(d + 1) * hd] = kd.astype(kd_o.dtype)
        b_o[:, d * hd:(d + 1) * hd] = (kk * a_d).astype(b_o.dtype)
        kd_sum = kd_sum + kd
    bonus_o[...] = (_seg_sum(r * kd_sum * rk_ref[...], seg) * v).astype(bonus_o.dtype)


def _block_diag2(w):
    z = jnp.zeros_like(w[0])
    return jnp.concatenate([jnp.concatenate([w[0], z], axis=1), jnp.concatenate([z, w[1]], axis=1)], axis=0)


def _head_seg_matrix(n, head):
    idx = jnp.arange(n) // head
    return (idx[:, None] == idx[None, :]).astype(BF16)


def _rwkv_prep(zr, lp, tile, n_lat_rows, seq, ctx_len):
    m, width = zr.shape
    hd = RWKV_DIM
    full = lambda a: pl.BlockSpec(a.shape, lambda i: (0,) * a.ndim)
    shift_mat = jnp.concatenate([jnp.eye(tile, k=-1, dtype=BF16), jnp.eye(tile, k=1, dtype=BF16)], axis=0)
    consts = [shift_mat, lp['rwkv_shift'],
              _block_diag2(lp['rwkv_w2']).astype(BF16), _block_diag2(lp['rwkv_a2']).astype(BF16),
              lp['rwkv_g2'].astype(BF16),
              lp['rwkv_w0'].reshape(1, 2 * hd), lp['rwkv_a0'].reshape(1, 2 * hd),
              lp['rwkv_kk'].reshape(1, hd), lp['rwkv_ka'].reshape(1, hd), lp['rwkv_rk'].reshape(1, hd),
              _head_seg_matrix(hd, RWKV_HEAD_DIM)]
    one = jax.ShapeDtypeStruct((m, hd), BF16)
    two = jax.ShapeDtypeStruct((m, 2 * hd), BF16)
    ospec = lambda n: pl.BlockSpec((tile, n), lambda i: (i, 0))
    return pl.pallas_call(
        functools.partial(_rwkv_prep_kernel, tile=tile, n_lat_rows=n_lat_rows, seq=seq, ctx_len=ctx_len),
        grid=(m // tile,),
        in_specs=_halo_specs(tile, width, m) + [full(a) for a in consts],
        out_specs=[ospec(hd)] * 5 + [ospec(2 * hd)] * 3,
        out_shape=[one] * 5 + [jax.ShapeDtypeStruct((m, 2 * hd), F32), two, two],
        compiler_params=_cparams("parallel"),
        name="rwkv_prep",
    )(zr, zr, zr, *consts)


def _split3(x):
    h1 = x.astype(BF16)
    r1 = x - h1.astype(F32)
    h2 = r1.astype(BF16)
    h3 = (r1 - h2.astype(F32)).astype(BF16)
    return h1, h2, h3


def _scan_kernel(*refs):
    ins = (refs[0:6], refs[6:12])
    y_refs = refs[12:14]
    s_ref = refs[14]
    c_len = SCAN_CHUNK
    n_sub = refs[0].shape[0] // c_len
    n_pairs = RWKV_DIM // LANES

    @pl.when(pl.program_id(1) == 0)
    def _():
        s_ref[...] = jnp.zeros_like(s_ref)

    row = lax.broadcasted_iota(jnp.int32, (c_len, c_len), 0)
    col = lax.broadcasted_iota(jnp.int32, (c_len, c_len), 1)
    row_c = lax.broadcasted_iota(jnp.int32, (c_len, LANES), 0)
    col_c = lax.broadcasted_iota(jnp.int32, (c_len, LANES), 1) & (c_len - 1)
    row_4 = lax.broadcasted_iota(jnp.int32, (c_len, 2 * LANES), 0)
    col_4 = lax.broadcasted_iota(jnp.int32, (c_len, 2 * LANES), 1) & (c_len - 1)
    eye_c = jnp.where(row_c == col_c, 1.0, 0.0)
    head0 = lax.broadcasted_iota(jnp.int32, (c_len, LANES), 1) < RWKV_HEAD_DIM
    srow = lax.broadcasted_iota(jnp.int32, (LANES, LANES), 0) // RWKV_HEAD_DIM
    scol = lax.broadcasted_iota(jnp.int32, (LANES, LANES), 1) // RWKV_HEAD_DIM
    same_head = srow == scol

    def by_head(x):
        zero = jnp.zeros_like(x)
        return jnp.concatenate([jnp.where(head0, x, zero), jnp.where(head0, zero, x)], axis=0)

    masks = []
    for d in range(2):
        ahead = (row - col) if d == 0 else (col - row)
        ahead_c = (row_c - col_c) if d == 0 else (col_c - row_c)
        ahead_4 = (row_4 - col_4) if d == 0 else (col_4 - row_4)
        tri = jnp.where(ahead >= 0, 1.0, 0.0).astype(BF16)
        pair_masks = []
        for sh in range(int(math.log2(c_len))):
            same_pair = (row_c >> (sh + 1)) == (col_c >> (sh + 1))
            pair_masks.append(same_pair & ((row_c >> sh) != (col_c >> sh)) & (ahead_c > 0))
        masks.append(dict(pair=pair_masks, strict=ahead_c > 0, incl4=ahead_4 >= 0, tri=tri))

    def scaled_operands(d, g):
        r_ref, lw_ref, k_ref, v_ref, kk_ref, b_ref = ins[d]
        rows = pl.ds(g * c_len, c_len)
        tri = masks[d]['tri']
        lw = lw_ref[rows, :]
        h1, h2, h3 = _split3(lw)
        c = _dot(tri, h1) + _dot(tri, h2) + _dot(tri, h3)
        tot = jnp.sum(lw, axis=0, keepdims=True)
        e_neg = jnp.exp(-c)
        dec = jnp.exp(tot - c)
        f32 = lambda ref: ref[rows, :].astype(F32)
        b_f, k_f = f32(b_ref), f32(k_ref)
        return dict(
            a_t=-f32(kk_ref) * jnp.exp(c - lw), r_t=f32(r_ref) * jnp.exp(c),
            b_t=(b_f * e_neg).astype(BF16), k_t=(k_f * e_neg).astype(BF16),
            b_p=(b_f * dec).astype(BF16), k_p=(k_f * dec).astype(BF16),
            p_tot=jnp.exp(tot), v=v_ref[rows, :])

    lanes_of = lambda p: slice(p * LANES, (p + 1) * LANES)
    sub_of = lambda d, step: step if d == 0 else n_sub - 1 - step
    state = {(d, p): s_ref[d, p] for d in range(2) for p in range(n_pairs)}
    wave_len = min(SCAN_WAVE, n_sub)

    n_waves = n_sub // wave_len
    wave_subs = lambda wave: [(d, sub_of(d, s)) for s in range(wave * wave_len, (wave + 1) * wave_len)
                              for d in range(2)]
    next_ops = {dg: scaled_operands(*dg) for dg in wave_subs(0)}
    for wave in range(n_waves):
        steps = range(wave * wave_len, (wave + 1) * wave_len)
        ops, next_ops = next_ops, {}
        pending = wave_subs(wave + 1) if wave + 1 < n_waves else []
        groups = [(d, g, p) for (d, g) in ops for p in range(n_pairs)]

        scores, v_bd = {}, {}
        for gr in groups:
            d, g, p = gr
            sl = lanes_of(p)
            lhs = jnp.concatenate([ops[d, g]['a_t'][:, sl], ops[d, g]['r_t'][:, sl]], axis=0).astype(BF16)
            rhs = jnp.concatenate([by_head(ops[d, g]['b_t'][:, sl]), by_head(ops[d, g]['k_t'][:, sl])], axis=0)
            scores[gr] = _dot_nt(lhs, rhs)
            v_bd[gr] = by_head(ops[d, g]['v'][:, sl])
        w0, l_mat, x_inv = {}, {}, {}
        for gr in groups:
            d = gr[0]
            top = scores[gr][:c_len]
            l_mat[gr] = top[:, :LANES]
            w0[gr] = _dot(jnp.where(masks[d]['strict'], top[:, LANES:], 0.0).astype(BF16), v_bd[gr])
            x_inv[gr] = eye_c + jnp.where(masks[d]['pair'][0], l_mat[gr], 0.0)

        for level in range(1, int(math.log2(c_len))):
            t1 = {}
            for gr in groups:
                pm = masks[gr[0]]['pair'][level]
                t1[gr] = _dot(jnp.where(pm, l_mat[gr], 0.0).astype(BF16), by_head(x_inv[gr].astype(BF16)))
            for gr in groups:
                x_inv[gr] = x_inv[gr] + _dot(x_inv[gr].astype(BF16), by_head(t1[gr].astype(BF16)))
            if pending:
                dg = pending.pop(0)
                next_ops[dg] = scaled_operands(*dg)
        for dg in pending:
            next_ops[dg] = scaled_operands(*dg)
        xa_u0 = {}
        for gr in groups:
            d, g, p = gr
            rhs = jnp.concatenate([by_head(ops[d, g]['a_t'][:, lanes_of(p)].astype(BF16)),
                                   by_head(w0[gr].astype(BF16))], axis=1)
            xa_u0[gr] = _dot(x_inv[gr].astype(BF16), rhs)
        p_row_of = {dg: i for i, dg in enumerate(ops)}
        p_all = jnp.concatenate([ops[dg]['p_tot'] for dg in ops]
                                + [jnp.zeros((LANES - len(ops), RWKV_DIM), F32)], axis=0)
        p_t = [jnp.transpose(p_all[:, lanes_of(p)]) for p in range(n_pairs)]
        bk_t = {}
        for d, g, p in groups:
            sl = lanes_of(p)
            bk_t[d, g, p] = jnp.transpose(
                jnp.concatenate([ops[d, g]['b_p'][:, sl], ops[d, g]['k_p'][:, sl]], axis=0))
        lhs_t, p_col = {}, {}
        for gr in groups:
            d, g, p = gr
            lhs_t[gr] = jnp.concatenate([xa_u0[gr][:, :LANES].astype(BF16),
                                         ops[d, g]['r_t'][:, lanes_of(p)].astype(BF16)], axis=0)
            r_idx = p_row_of[d, g]
            p_col[gr] = jnp.broadcast_to(p_t[p][:, r_idx:r_idx + 1], (LANES, LANES))

        for step in steps:
            live = [(d, sub_of(d, step), p) for d in range(2) for p in range(n_pairs)]
            prod = {gr: _dot(lhs_t[gr], state[gr[0], gr[2]].astype(BF16)) for gr in live}
            u = {gr: (prod[gr][:c_len] + xa_u0[gr][:, LANES:]).astype(BF16) for gr in live}
            upd = {gr: _dot(bk_t[gr],
                            jnp.concatenate([u[gr], ops[gr[0], gr[1]]['v'][:, lanes_of(gr[2])]], axis=0))
                   for gr in live}
            for gr in live:
                d, g, p = gr
                state[d, p] = p_col[gr] * state[d, p] + jnp.where(same_head, upd[gr], 0.0)
            for gr in live:
                d, g, p = gr
                y = _dot(jnp.where(masks[d]['incl4'], scores[gr][c_len:], 0.0).astype(BF16),
                         jnp.concatenate([by_head(u[gr]), v_bd[gr]], axis=0))
                y_refs[d][pl.ds(g * c_len, c_len), lanes_of(p)] = prod[gr][c_len:] + y
    for (d, p), s_new in state.items():
        s_ref[d, p] = s_new


def _rwkv_scan(r, lw, kd, v, kk, bvec, batch, seq, ctx_len):
    m, hd = r.shape
    n_sub = _pick_tile(SCAN_SUBCHUNKS, ctx_len // SCAN_CHUNK, seq // SCAN_CHUNK)
    c_len = n_sub * SCAN_CHUNK
    n_ctx, n_lat = ctx_len // c_len, seq // c_len
    ctx0 = batch * seq // c_len

    def rowblk(d):
        def fn(b, j):
            jc = j if d == 0 else n_ctx - 1 - j
            jl = (j - n_ctx) if d == 0 else n_lat - 1 - (j - n_ctx)
            return jnp.where(j < n_ctx, ctx0 + b * n_ctx + jc, b * n_lat + jl)
        return fn

    in_specs, args = [], []
    for d in range(2):
        shared = pl.BlockSpec((c_len, hd), lambda b, j, f=rowblk(d): (f(b, j), 0))
        per_dir = pl.BlockSpec((c_len, hd), lambda b, j, f=rowblk(d), d=d: (f(b, j), d))
        in_specs += [shared, per_dir, per_dir, shared, shared, per_dir]
        args += [r, lw, kd, v, kk, bvec]
    out = jax.ShapeDtypeStruct((m, hd), F32)
    return pl.pallas_call(
        _scan_kernel,
        grid=(batch, n_ctx + n_lat),
        in_specs=in_specs,
        out_specs=[pl.BlockSpec((c_len, hd), lambda b, j, f=rowblk(d): (f(b, j), 0)) for d in range(2)],
        out_shape=[out, out],
        scratch_shapes=[pltpu.VMEM((2, hd // LANES, LANES, LANES), F32)],
        compiler_params=_cparams("parallel", "arbitrary"),
        name="rwkv_scan",
    )(*args)


def _merge_kernel(x_ref, yc_ref, ya_ref, yf_ref, yb_ref, bonus_ref, rg_ref, gg_ref, gb_ref, seg_ref,
                  gt_ref, pc_ref, pa_ref, pr_ref, wo_ref, gate_ref, o_ref):
    d = x_ref.shape[1]
    seg = seg_ref[...]
    ys = yf_ref[...] + yb_ref[...]
    inv_n = 1.0 / RWKV_HEAD_DIM
    mu = _seg_sum(ys, seg) * inv_n
    dlt = ys - mu
    var = _seg_sum(dlt * dlt, seg) * inv_n
    yn = dlt * lax.rsqrt(var + GN_EPS) * gg_ref[...] + gb_ref[...]
    y_rw = ((yn + bonus_ref[...].astype(F32)) * rg_ref[...].astype(F32)).astype(BF16)
    g = jax.nn.sigmoid(gt_ref[...].astype(F32))
    mix = (g[:, 0:d] * _dot(yc_ref[...], pc_ref[...])
           + g[:, d:2 * d] * _dot(ya_ref[...], pa_ref[...])
           + g[:, 2 * d:] * _dot(y_rw, pr_ref[...]))
    o_ref[...] = x_ref[...] + gate_ref[...] * _dot(mix.astype(BF16), wo_ref[...])


def _merge(x, rows, yc, ya, yf, yb, bonus, rg, gn_g, gn_b, gates, pc, pa, pr, wo, mod_tab, tm, row_of_tile):
    d = x.shape[1]
    hd = yf.shape[1]
    seg = _head_seg_matrix(hd, RWKV_HEAD_DIM)
    gn_g, gn_b = gn_g.reshape(1, hd), gn_b.reshape(1, hd)
    tok = lambda a: pl.BlockSpec((tm, a.shape[1]), lambda i: (i, 0))
    full = lambda a: pl.BlockSpec(a.shape, lambda i: (0, 0))
    return pl.pallas_call(
        _merge_kernel,
        grid=(rows // tm,),
        in_specs=[tok(x), tok(yc), tok(ya), tok(yf), tok(yb), tok(bonus), tok(rg), full(gn_g), full(gn_b),
                  full(seg), tok(gates), full(pc), full(pa), full(pr), full(wo),
                  pl.BlockSpec((None, None, 1, d), lambda i: (row_of_tile(i), 2, 0, 0))],
        out_specs=tok(x),
        out_shape=jax.ShapeDtypeStruct((rows, d), F32),
        compiler_params=_cparams("parallel"),
        name="merge_out_proj",
    )(x, yc, ya, yf, yb, bonus, rg, gn_g, gn_b, seg, gates, pc, pa, pr, wo, mod_tab)


def _mlp_kernel(*refs, final):
    if final:
        x_ref, g_ref, sh_ref, sc_ref, gate_ref, w1_ref, w2_ref, fg_ref, o_ref, h_ref, acc_ref = refs
    else:
        x_ref, g_ref, sh_ref, sc_ref, gate_ref, w1_ref, w2_ref, o_ref, h_ref, acc_ref = refs
    f = pl.program_id(1)

    @pl.when(f == 0)
    def _():
        h_ref[...] = _modulated_norm(x_ref[...], g_ref[...], sh_ref[...], sc_ref[...]).astype(BF16)
        acc_ref[...] = jnp.zeros_like(acc_ref)

    t = jnp.maximum(_dot(h_ref[...], w1_ref[...]), 0.0)
    acc_ref[...] += _dot((t * t).astype(BF16), w2_ref[...])

    @pl.when(f == pl.num_programs(1) - 1)
    def _():
        y = x_ref[...] + gate_ref[...] * acc_ref[...]
        if final:
            y = y * lax.rsqrt(jnp.mean(y * y, axis=-1, keepdims=True) + NORM_EPS) * fg_ref[...]
        o_ref[...] = y


def _mlp(x, rows, gain, mod_tab, w1, w2, final_g, tm, tf, row_of_tile):
    d = x.shape[1]
    dff = w1.shape[1]
    final = final_g is not None
    tab = lambda which: pl.BlockSpec((None, None, 1, d), lambda i, f: (row_of_tile(i), which, 0, 0))
    in_specs = [pl.BlockSpec((tm, d), lambda i, f: (i, 0)),
                pl.BlockSpec((1, d), lambda i, f: (0, 0)),
                tab(3), tab(4), tab(5),
                pl.BlockSpec((d, tf), lambda i, f: (0, f)),
                pl.BlockSpec((tf, d), lambda i, f: (f, 0))]
    args = [x, gain.reshape(1, d), mod_tab, mod_tab, mod_tab, w1, w2]
    if final:
        in_specs.append(pl.BlockSpec((1, d), lambda i, f: (0, 0)))
        args.append(final_g.reshape(1, d))
    return pl.pallas_call(
        functools.partial(_mlp_kernel, final=final),
        grid=(rows // tm, dff // tf),
        in_specs=in_specs,
        out_specs=pl.BlockSpec((tm, d), lambda i, f: (i, 0)),
        out_shape=jax.ShapeDtypeStruct((rows, d), F32),
        scratch_shapes=[pltpu.VMEM((tm, d), BF16), pltpu.VMEM((tm, d), F32)],
        compiler_params=_cparams("parallel", "arbitrary"),
        name="mlp_final" if final else "mlp",
    )(*args)


def kernel(x, c, ctx, c_ctx, mod_w, mod_b, norm1_g, norm2_g, w_in, conv_dw_w, conv_dw_b, conv_ln_g, conv_ln_b, p_conv, att_lq1, att_lk1, att_lq2, att_lk2, att_subln_g, p_att, rwkv_shift, rwkv_w0, rwkv_w2, rwkv_a0, rwkv_a2, rwkv_g2, rwkv_kk, rwkv_ka, rwkv_rk, rwkv_gn_g, rwkv_gn_b, p_rwkv, w_out, mlp_w1, mlp_w2, final_g):
    batch, seq, d = x.shape
    ctx_len = ctx.shape[1]
    depth = mod_w.shape[0]
    n_lat = batch * seq
    conv_w = 2 * conv_dw_w.shape[2]
    att_w = 3 * ATT_HEADS * ATT_VAL_DIM
    rw_w = 3 * RWKV_DIM + 2 * DECAY_LORA + 2 * ICLR_LORA + GATE_LORA
    assert w_in.shape[2] == conv_w + att_w + rw_w + 3 * d
    assert seq % SCAN_CHUNK == 0 and ctx_len % SCAN_CHUNK == 0 and (batch * seq) % ctx_len == 0

    tm = _pick_tile(1024, seq, batch * ctx_len)
    tmm = _pick_tile(512, seq, batch * ctx_len)
    tc = _pick_tile(256, seq, ctx_len)
    tiles_per_seq = seq // tm
    row_of_tile = lambda i: jnp.minimum(i // tiles_per_seq, batch)
    row_of_tile_m = lambda i: jnp.minimum(i // (seq // tmm), batch)

    xs = jnp.concatenate([x.reshape(n_lat, d), ctx.reshape(batch * ctx_len, d)], axis=0)
    pad = (-(batch + 1)) % 8
    c_all = jnp.concatenate([c, c_ctx[None, :], jnp.zeros((pad, d), F32)], axis=0)
    rope_tabs = _rope_tables(seq)

    for l in range(depth):
        last = l == depth - 1
        lambda_init = 0.8 - 0.6 * math.exp(-0.3 * l)
        mod = _modulation(c_all, mod_w[l], mod_b[l])
        mod_tab = mod[:batch + 1].reshape(batch + 1, 6, 1, d)

        wl = w_in[l].astype(BF16)
        o1, o2, o3 = conv_w, conv_w + att_w, conv_w + att_w + rw_w
        zc, h1 = _norm_project(xs, norm1_g[l], mod_tab, 0, wl[:, :o1], tm, row_of_tile, "norm_proj_conv")
        qkv = _project(h1, wl[:, o1:o2], tm, "proj_qkv")
        zr = _project(h1, wl[:, o2:o3], tm, "proj_rwkv")
        gates = _project(h1, wl[:, o3:], tm, "proj_gates")

        y_conv = _conv_branch(zc, conv_dw_w[l], conv_dw_b[l], conv_ln_g[l], conv_ln_b[l],
                              tc, n_lat, seq, ctx_len)

        lam_vecs = jnp.stack([att_lq1[l], att_lk1[l], att_lq2[l], att_lk2[l]])
        y_att = _attention(qkv, rope_tabs, lam_vecs, att_subln_g[l], lambda_init, batch, seq, ctx_len)
        if not last:
            y_att = _attention(qkv, None, lam_vecs, att_subln_g[l], lambda_init, batch, seq, ctx_len, y_att)

        lp = dict(rwkv_shift=rwkv_shift[l], rwkv_w0=rwkv_w0[l], rwkv_w2=rwkv_w2[l], rwkv_a0=rwkv_a0[l],
                  rwkv_a2=rwkv_a2[l], rwkv_g2=rwkv_g2[l], rwkv_kk=rwkv_kk[l], rwkv_ka=rwkv_ka[l],
                  rwkv_rk=rwkv_rk[l])
        r, v, kk, g, bonus, lw, kd, bvec = _rwkv_prep(zr, lp, tc, n_lat, seq, ctx_len)
        yf, yb = _rwkv_scan(r, lw, kd, v, kk, bvec, batch, seq, ctx_len)

        rows = n_lat if last else xs.shape[0]
        xs = _merge(xs, rows, y_conv, y_att, yf, yb, bonus, g, rwkv_gn_g[l], rwkv_gn_b[l], gates,
                    p_conv[l].astype(BF16), p_att[l].astype(BF16),
                    p_rwkv[l].astype(BF16), w_out[l].astype(BF16), mod_tab, tmm, row_of_tile_m)
        xs = _mlp(xs, rows, norm2_g[l], mod_tab, mlp_w1[l].astype(BF16), mlp_w2[l].astype(BF16),
                  final_g if last else None, tm, _pick_tile(MLP_FF_TILE, mlp_w1.shape[2]), row_of_tile)
    return xs.reshape(batch, seq, d)
```

```python
import functools
import math

import jax
import jax.numpy as jnp
from jax import lax
from jax.experimental import pallas as pl
from jax.experimental.pallas import tpu as pltpu

F32 = jnp.float32
BF16 = jnp.bfloat16

GRID_W = 64
NORM_EPS = 1e-6
CONV_WIDTH = 31
LN_EPS = 1e-5
ATT_HEADS = 4
ATT_HEAD_DIM = 64
ATT_VAL_DIM = 2 * ATT_HEAD_DIM
ROPE_BASE = 10000.0
SUBLN_EPS = 1e-5
RWKV_HEADS = 8
RWKV_HEAD_DIM = 64
RWKV_DIM = RWKV_HEADS * RWKV_HEAD_DIM
DECAY_LORA = 64
ICLR_LORA = 64
GATE_LORA = 128
GN_EPS = 64e-5

LANES = 128
SUBLANES = 8
MXU_WIDTH = 256
HALO = 16
SCAN_CHUNK = 64
SCAN_SUBCHUNKS = 4
SCAN_WAVE = 2
MLP_FF_TILE = 1024
PROJ_ROW_PARTS = 2
ATT_Q_TILE = 1024
ATT_ROW_PARTS = 8
VMEM_LIMIT = 56 << 20


def _cparams(*sem):
    return pltpu.CompilerParams(dimension_semantics=sem, vmem_limit_bytes=VMEM_LIMIT)


def _pick_tile(limit, *sizes):
    t = limit
    while any(s % t for s in sizes):
        t //= 2
    return t


def _dot(a, b):
    return jnp.dot(a, b, preferred_element_type=F32)


def _dot_nt(a, b):
    return lax.dot_general(a, b, (((1,), (1,)), ((), ())), preferred_element_type=F32)


def _dot_tn(a, b):
    return lax.dot_general(a, b, (((0,), (0,)), ((), ())), preferred_element_type=F32)


def _seg_sum(x, seg):
    w = seg.shape[0]
    hi = x.astype(BF16)
    lo = (x - hi.astype(F32)).astype(BF16)
    return jnp.concatenate([_dot(hi[:, j:j + w], seg) + _dot(lo[:, j:j + w], seg)
                            for j in range(0, x.shape[1], w)], axis=1)


def _modulated_norm(x, gain, shift, scale):
    y = x * lax.rsqrt(jnp.mean(x * x, axis=-1, keepdims=True) + NORM_EPS)
    return (y * gain) * (1.0 + scale) + shift


def _mod_kernel(c_ref, w_ref, b_ref, o_ref):
    c = c_ref[...]
    s = c * jax.nn.sigmoid(c)
    o_ref[...] = _dot(s.astype(BF16), w_ref[...].astype(BF16)) + b_ref[...]


def _modulation(c_all, mod_w, mod_b):
    rows, d = c_all.shape
    n = mod_w.shape[1]
    tn = _pick_tile(1024, n // 6)
    return pl.pallas_call(
        _mod_kernel,
        grid=(n // tn,),
        in_specs=[pl.BlockSpec((rows, d), lambda j: (0, 0)),
                  pl.BlockSpec((d, tn), lambda j: (0, j)),
                  pl.BlockSpec((1, tn), lambda j: (0, j))],
        out_specs=pl.BlockSpec((rows, tn), lambda j: (0, j)),
        out_shape=jax.ShapeDtypeStruct((rows, n), F32),
        compiler_params=_cparams("arbitrary"),
        name="adaln_modulation",
    )(c_all, mod_w, mod_b.reshape(1, n))


def _row_specs(parts, tile):
    lat, ctx_rows = parts
    cols = lat.shape[1]
    if ctx_rows is None:
        return [pl.BlockSpec((tile, cols), lambda i: (i, 0))], [lat]
    n_lat_tiles = lat.shape[0] // tile
    return ([pl.BlockSpec((tile, cols), lambda i: (jnp.minimum(i, n_lat_tiles - 1), 0)),
             pl.BlockSpec((tile, cols), lambda i: (jnp.maximum(i - n_lat_tiles, 0), 0))], [lat, ctx_rows])


def _tile_rows(refs, n_lat_tiles, rows=slice(None)):
    if len(refs) == 1:
        return refs[0][rows, :]
    return jnp.where(pl.program_id(0) < n_lat_tiles, refs[0][rows, :], refs[1][rows, :])


def _norm_proj_kernel(*refs, n_x, n_lat_tiles):
    x_refs = refs[:n_x]
    g_ref, sh_ref, sc_ref, w_ref, o_ref, h_ref = refs[n_x:]
    part = o_ref.shape[0] // PROJ_ROW_PARTS
    for i in range(PROJ_ROW_PARTS):
        rows = pl.ds(i * part, part)
        x = _tile_rows(x_refs, n_lat_tiles, rows)
        h = _modulated_norm(x, g_ref[...], sh_ref[...], sc_ref[...]).astype(BF16)
        h_ref[rows, :] = h
        o_ref[rows, :] = _dot(h, w_ref[...]).astype(o_ref.dtype)


def _norm_project(x_parts, gain, mod_tab, which_shift, w, tm, row_of_tile, name):
    m = sum(a.shape[0] for a in x_parts if a is not None)
    d = x_parts[0].shape[1]
    n = w.shape[1]
    x_specs, x_args = _row_specs(x_parts, tm)
    tab = lambda which: pl.BlockSpec((None, None, 1, d), lambda i: (row_of_tile(i), which, 0, 0))
    return pl.pallas_call(
        functools.partial(_norm_proj_kernel, n_x=len(x_args), n_lat_tiles=x_parts[0].shape[0] // tm),
        grid=(m // tm,),
        in_specs=x_specs + [pl.BlockSpec((1, d), lambda i: (0, 0)),
                            tab(which_shift), tab(which_shift + 1),
                            pl.BlockSpec((d, n), lambda i: (0, 0))],
        out_specs=[pl.BlockSpec((tm, n), lambda i: (i, 0)), pl.BlockSpec((tm, d), lambda i: (i, 0))],
        out_shape=[jax.ShapeDtypeStruct((m, n), BF16), jax.ShapeDtypeStruct((m, d), BF16)],
        compiler_params=_cparams("parallel"),
        name=name,
    )(*x_args, gain.reshape(1, d), mod_tab, mod_tab, w)


def _proj_kernel(h_ref, w_ref, o_ref):
    o_ref[...] = _dot(h_ref[...], w_ref[...]).astype(o_ref.dtype)


def _project(h, w, tm, name):
    m, d = h.shape
    n = w.shape[1]
    return pl.pallas_call(
        _proj_kernel,
        grid=(m // tm,),
        in_specs=[pl.BlockSpec((tm, d), lambda i: (i, 0)), pl.BlockSpec((d, n), lambda i: (0, 0))],
        out_specs=pl.BlockSpec((tm, n), lambda i: (i, 0)),
        out_shape=jax.ShapeDtypeStruct((m, n), BF16),
        compiler_params=_cparams("parallel"),
        name=name,
    )(h, w)


def _seq_edges(i, tile, n_lat_rows, seq, ctx_len):
    n_lat_tiles = n_lat_rows // tile
    is_lat = i < n_lat_tiles
    per = jnp.where(is_lat, seq // tile, ctx_len // tile)
    pos = jnp.where(is_lat, i, i - n_lat_tiles) % per
    return pos == 0, pos == per - 1


def _conv_kernel(z_ref, zp_ref, zn_ref, w_ref, b_ref, lg_ref, lb_ref, o_ref, ext_ref, rot_ref,
                 *, tile, n_lat_rows, seq, ctx_len):
    first, last = _seq_edges(pl.program_id(0), tile, n_lat_rows, seq, ctx_len)
    cd = o_ref.shape[1]

    def glu(z):
        z = z.astype(F32)
        return z[:, :cd] * jax.nn.sigmoid(z[:, cd:])

    ext_ref[0:HALO, :] = jnp.where(first, 0.0, glu(zp_ref[...]))
    ext_ref[HALO:HALO + tile, :] = glu(z_ref[...])
    ext_ref[HALO + tile:, :] = jnp.where(last, 0.0, glu(zn_ref[...]))
    span = rot_ref.shape[1]
    for r in range(SUBLANES):
        rot_ref[r] = ext_ref[pl.ds(r, span), :]
    half = CONV_WIDTH // 2
    acc = jnp.zeros((tile, cd), F32)
    for k in range(CONV_WIDTH):
        start = HALO - half + k
        acc = acc + w_ref[k:k + 1, :] * rot_ref[start % SUBLANES, pl.ds(start - start % SUBLANES, tile), :]
    u = acc + b_ref[...]
    mu = jnp.mean(u, axis=-1, keepdims=True)
    dlt = u - mu
    var = jnp.mean(dlt * dlt, axis=-1, keepdims=True)
    y = dlt * lax.rsqrt(var + LN_EPS) * lg_ref[...] + lb_ref[...]
    o_ref[...] = (y * jax.nn.sigmoid(y)).astype(o_ref.dtype)


def _halo_specs(tile, width, m):
    per = tile // HALO
    last_blk = m // HALO - 1
    return [pl.BlockSpec((tile, width), lambda i: (i, 0)),
            pl.BlockSpec((HALO, width), lambda i: (jnp.maximum(i * per - 1, 0), 0)),
            pl.BlockSpec((HALO, width), lambda i: (jnp.minimum((i + 1) * per, last_blk), 0))]


def _conv_branch(zc, dw_w, dw_b, ln_g, ln_b, tile, n_lat_rows, seq, ctx_len):
    m, width = zc.shape
    cd = width // 2
    vec = pl.BlockSpec((1, cd), lambda i: (0, 0))
    return pl.pallas_call(
        functools.partial(_conv_kernel, tile=tile, n_lat_rows=n_lat_rows, seq=seq, ctx_len=ctx_len),
        grid=(m // tile,),
        in_specs=_halo_specs(tile, width, m) + [pl.BlockSpec((CONV_WIDTH, cd), lambda i: (0, 0)),
                                                vec, vec, vec],
        out_specs=pl.BlockSpec((tile, cd), lambda i: (i, 0)),
        out_shape=jax.ShapeDtypeStruct((m, cd), BF16),
        scratch_shapes=[pltpu.VMEM((tile + 2 * HALO, cd), F32),
                        pltpu.VMEM((SUBLANES, tile + 2 * HALO - SUBLANES, cd), F32)],
        compiler_params=_cparams("parallel"),
        name="conv_branch",
    )(zc, zc, zc, dw_w, dw_b.reshape(1, cd), ln_g.reshape(1, cd), ln_b.reshape(1, cd))


def _rope(x, cos, sin_next, sin_prev):
    return x * cos + pltpu.roll(x, LANES - 1, 1) * sin_next + pltpu.roll(x, 1, 1) * sin_prev


def _attn_kernel(*refs, lambda_init, n_ctx, with_lat):
    if with_lat:
        (q_ref, kc_ref, vc_ref, kl_ref, vl_ref, cq_ref, snq_ref, spq_ref, ck_ref, snk_ref, spk_ref,
         lam_ref, g_ref, o_ref, ks_ref, vs_ref) = refs
    else:
        q_ref, kc_ref, vc_ref, lam_ref, g_ref, o_ref, ks_ref, vs_ref = refs
    dv = ATT_VAL_DIM

    @pl.when(pl.program_id(2) == 0)
    def _():
        vs_ref[:, dv:] = jnp.ones((vs_ref.shape[0], dv), BF16)
        ks_ref[0:n_ctx, :] = kc_ref[...]
        vs_ref[0:n_ctx, 0:dv] = vc_ref[...]
        if with_lat:
            k = _rope(kl_ref[...].astype(F32), ck_ref[...], snk_ref[...], spk_ref[...])
            ks_ref[n_ctx:, :] = k.astype(BF16)
            vs_ref[n_ctx:, 0:dv] = vl_ref[...]

    tq = q_ref.shape[0]
    q = q_ref[...].astype(F32)
    if with_lat:
        q = _rope(q, cq_ref[...], snq_ref[...], spq_ref[...])
    q = q * (math.log2(math.e) / math.sqrt(ATT_HEAD_DIM))
    first_map = lax.broadcasted_iota(jnp.int32, q.shape, 1) < ATT_HEAD_DIM
    qs = jnp.concatenate([jnp.where(first_map, q, 0.0), jnp.where(first_map, 0.0, q)], axis=0).astype(BF16)
    part = (2 * tq) // ATT_ROW_PARTS
    scores = lambda i: _dot_nt(qs[i * part:(i + 1) * part], ks_ref[...])
    s_next = scores(0)
    outs = []
    for i in range(ATT_ROW_PARTS):
        s = s_next
        if i + 1 < ATT_ROW_PARTS:
            s_next = scores(i + 1)
        p = jnp.exp2((s - jnp.max(s, axis=-1, keepdims=True)).astype(BF16))
        outs.append(_dot(p, vs_ref[...]))
    ol = jnp.concatenate(outs, axis=0)
    o = ol[:, 0:dv] / ol[:, dv:]
    lq = lam_ref[...]
    lam = (jnp.exp(jnp.sum(lq[0:1] * lq[1:2], axis=-1, keepdims=True))
           - jnp.exp(jnp.sum(lq[2:3] * lq[3:4], axis=-1, keepdims=True)) + lambda_init)
    a = o[:tq] - lam * o[tq:]
    y = a * lax.rsqrt(jnp.mean(a * a, axis=-1, keepdims=True) + SUBLN_EPS) * g_ref[...]
    o_ref[...] = (y * (1.0 - lambda_init)).astype(o_ref.dtype)


def _attention(qkv, rope_tabs, lam_vecs, subln_g, lambda_init, batch, seq, ctx_len, latent_queries):
    h_ = ATT_HEADS
    ctx_blk0 = batch * seq // ctx_len
    tq = _pick_tile(ATT_Q_TILE, seq if latent_queries else ctx_len)
    nq = (seq if latent_queries else ctx_len) // tq
    q0 = 0 if latent_queries else batch * seq // tq
    blk = lambda rows, fn: pl.BlockSpec((rows, LANES), fn)
    in_specs = [blk(tq, lambda b, h, i: (q0 + b * nq + i, h)),
                blk(ctx_len, lambda b, h, i: (ctx_blk0 + b, h_ + h)),
                blk(ctx_len, lambda b, h, i: (ctx_blk0 + b, 2 * h_ + h))]
    args = [qkv, qkv, qkv]
    n_keys = ctx_len
    if latent_queries:
        n_keys += seq
        in_specs += [blk(seq, lambda b, h, i: (b, h_ + h)), blk(seq, lambda b, h, i: (b, 2 * h_ + h))]
        in_specs += [blk(tq, lambda b, h, i: (i, 0))] * 3 + [blk(seq, lambda b, h, i: (0, 0))] * 3
        args += [qkv, qkv] + list(rope_tabs) + list(rope_tabs)
    in_specs += [pl.BlockSpec((4, ATT_HEAD_DIM), lambda b, h, i: (0, 0)),
                 pl.BlockSpec((1, ATT_VAL_DIM), lambda b, h, i: (0, 0))]
    args += [lam_vecs, subln_g.reshape(1, ATT_VAL_DIM)]
    out_rows = batch * (seq if latent_queries else ctx_len)
    return pl.pallas_call(
        functools.partial(_attn_kernel, lambda_init=lambda_init, n_ctx=ctx_len, with_lat=latent_queries),
        grid=(batch, h_, nq),
        in_specs=in_specs,
        out_specs=blk(tq, lambda b, h, i: (b * nq + i, h)),
        out_shape=jax.ShapeDtypeStruct((out_rows, h_ * ATT_VAL_DIM), BF16),
        scratch_shapes=[pltpu.VMEM((n_keys, LANES), BF16), pltpu.VMEM((n_keys, 2 * ATT_VAL_DIM), BF16)],
        compiler_params=_cparams("parallel", "parallel", "arbitrary"),
        name="diff_attention_latent" if latent_queries else "diff_attention_context",
    )(*args)


def _rope_tables(seq):
    t = jnp.arange(seq, dtype=jnp.int32)
    row = (t // GRID_W).astype(F32)
    col = (t % GRID_W).astype(F32)
    n_pairs = ATT_HEAD_DIM // 4
    inv_freq = ROPE_BASE ** (-jnp.arange(n_pairs, dtype=F32) / n_pairs)
    ang = jnp.concatenate([row[:, None] * inv_freq, col[:, None] * inv_freq], axis=-1)
    cos = jnp.repeat(jnp.cos(ang), 2, axis=-1)
    sin = jnp.repeat(jnp.sin(ang), 2, axis=-1)
    even = (jnp.arange(ATT_HEAD_DIM) % 2 == 0)[None, :]
    both = lambda a: jnp.concatenate([a, a], axis=-1)
    return both(cos), both(jnp.where(even, -sin, 0.0)), both(jnp.where(even, 0.0, sin))


def _rwkv_prep_kernel(z_ref, zp_ref, zn_ref, smat_ref, sh_ref, w2_ref, a2_ref, g2_ref, w0_ref, a0_ref,
                      kkw_ref, ka_ref, rk_ref, seg_ref,
                      r_o, v_o, kk_o, g_o, bonus_o, lw_o, kd_o, b_o,
                      *, tile, n_lat_rows, seq, ctx_len):
    first, last = _seq_edges(pl.program_id(0), tile, n_lat_rows, seq, ctx_len)
    hd = RWKV_DIM
    zc = z_ref[...]
    shifted = _dot(smat_ref[...], zc)
    prev_row = jnp.where(first, 0.0, zp_ref[...].astype(F32)[HALO - 1:HALO, :])
    next_row = jnp.where(last, 0.0, zn_ref[...].astype(F32)[0:1, :])
    t_idx = lax.broadcasted_iota(jnp.int32, (tile, 1), 0)
    z = (sh_ref[0:1, :] * jnp.where(t_idx == 0, prev_row, shifted[:tile])
         + sh_ref[1:2, :] * zc.astype(F32)
         + sh_ref[2:3, :] * jnp.where(t_idx == tile - 1, next_row, shifted[tile:]))
    seg = seg_ref[...]
    r = z[:, 0:hd]
    k = z[:, hd:2 * hd]
    v = z[:, 2 * hd:3 * hd]
    o = 3 * hd
    wd = z[:, o:o + 2 * DECAY_LORA]
    ad = z[:, o + 2 * DECAY_LORA:o + 2 * DECAY_LORA + 2 * ICLR_LORA]
    gd = z[:, o + 2 * DECAY_LORA + 2 * ICLR_LORA:]
    kkf = k * kkw_ref[...]
    kk = kkf * lax.rsqrt(jnp.maximum(_seg_sum(kkf * kkf, seg), 1e-24))
    g = _dot(jax.nn.sigmoid(gd).astype(BF16), g2_ref[...])
    wl = w0_ref[...] + _dot(jnp.tanh(wd).astype(BF16), w2_ref[...])
    al = a0_ref[...] + _dot(ad.astype(BF16), a2_ref[...])
    softplus = jnp.maximum(-wl, 0.0) + jnp.log(1.0 + jnp.exp(-jnp.abs(wl)))
    lw = -jnp.exp(-softplus - 0.5)
    a = jax.nn.sigmoid(al)
    r_o[...] = r.astype(r_o.dtype)
    v_o[...] = v.astype(v_o.dtype)
    kk_o[...] = kk.astype(kk_o.dtype)
    g_o[...] = g.astype(g_o.dtype)
    lw_o[...] = lw
    kd_sum = jnp.zeros_like(r)
    for d in range(2):
        a_d = a[:, d * hd:(d + 1) * hd]
        kd = k * (1.0 + (a_d - 1.0) * ka_ref[...])
        kd_o[:, d * hd:(d + 1) * hd] = kd.astype(kd_o.dtype)
        b_o[:, d * hd:(d + 1) * hd] = (kk * a_d).astype(b_o.dtype)
        kd_sum = kd_sum + kd
    bonus_o[...] = (_seg_sum(r * kd_sum * rk_ref[...], seg) * v).astype(bonus_o.dtype)


def _block_diag2(w):
    z = jnp.zeros_like(w[0])
    return jnp.concatenate([jnp.concatenate([w[0], z], axis=1), jnp.concatenate([z, w[1]], axis=1)], axis=0)


def _head_seg_matrix(n, head):
    idx = jnp.arange(n) // head
    return (idx[:, None] == idx[None, :]).astype(BF16)


def _rwkv_prep(zr, lp, tile, n_lat_rows, seq, ctx_len):
    m, width = zr.shape
    hd = RWKV_DIM
    full = lambda a: pl.BlockSpec(a.shape, lambda i: (0,) * a.ndim)
    shift_mat = jnp.concatenate([jnp.eye(tile, k=-1, dtype=BF16), jnp.eye(tile, k=1, dtype=BF16)], axis=0)
    consts = [shift_mat, lp['rwkv_shift'],
              _block_diag2(lp['rwkv_w2']).astype(BF16), _block_diag2(lp['rwkv_a2']).astype(BF16),
              lp['rwkv_g2'].astype(BF16),
              lp['rwkv_w0'].reshape(1, 2 * hd), lp['rwkv_a0'].reshape(1, 2 * hd),
              lp['rwkv_kk'].reshape(1, hd), lp['rwkv_ka'].reshape(1, hd), lp['rwkv_rk'].reshape(1, hd),
              _head_seg_matrix(min(hd, MXU_WIDTH), RWKV_HEAD_DIM)]
    one = jax.ShapeDtypeStruct((m, hd), BF16)
    two = jax.ShapeDtypeStruct((m, 2 * hd), BF16)
    ospec = lambda n: pl.BlockSpec((tile, n), lambda i: (i, 0))
    return pl.pallas_call(
        functools.partial(_rwkv_prep_kernel, tile=tile, n_lat_rows=n_lat_rows, seq=seq, ctx_len=ctx_len),
        grid=(m // tile,),
        in_specs=_halo_specs(tile, width, m) + [full(a) for a in consts],
        out_specs=[ospec(hd)] * 5 + [ospec(2 * hd)] * 3,
        out_shape=[one] * 5 + [jax.ShapeDtypeStruct((m, 2 * hd), F32), two, two],
        compiler_params=_cparams("parallel"),
        name="rwkv_prep",
    )(zr, zr, zr, *consts)


def _split3(x):
    h1 = x.astype(BF16)
    r1 = x - h1.astype(F32)
    h2 = r1.astype(BF16)
    h3 = (r1 - h2.astype(F32)).astype(BF16)
    return h1, h2, h3


def _scan_kernel(*refs):
    ins = (refs[0:6], refs[6:12])
    y_refs = refs[12:14]
    s_ref = refs[14]
    c_len = SCAN_CHUNK
    n_sub = refs[0].shape[0] // c_len
    n_pairs = RWKV_DIM // LANES

    @pl.when(pl.program_id(1) == 0)
    def _():
        s_ref[...] = jnp.zeros_like(s_ref)

    row = lax.broadcasted_iota(jnp.int32, (c_len, c_len), 0)
    col = lax.broadcasted_iota(jnp.int32, (c_len, c_len), 1)
    row_c = lax.broadcasted_iota(jnp.int32, (c_len, LANES), 0)
    col_c = lax.broadcasted_iota(jnp.int32, (c_len, LANES), 1) & (c_len - 1)
    row_4 = lax.broadcasted_iota(jnp.int32, (c_len, 2 * LANES), 0)
    col_4 = lax.broadcasted_iota(jnp.int32, (c_len, 2 * LANES), 1) & (c_len - 1)
    eye_c = jnp.where(row_c == col_c, 1.0, 0.0)
    head0 = lax.broadcasted_iota(jnp.int32, (c_len, LANES), 1) < RWKV_HEAD_DIM
    srow = lax.broadcasted_iota(jnp.int32, (LANES, LANES), 0) // RWKV_HEAD_DIM
    scol = lax.broadcasted_iota(jnp.int32, (LANES, LANES), 1) // RWKV_HEAD_DIM
    same_head = srow == scol

    def by_head(x):
        zero = jnp.zeros_like(x)
        return jnp.concatenate([jnp.where(head0, x, zero), jnp.where(head0, zero, x)], axis=0)

    masks = []
    for d in range(2):
        ahead = (row - col) if d == 0 else (col - row)
        ahead_c = (row_c - col_c) if d == 0 else (col_c - row_c)
        ahead_4 = (row_4 - col_4) if d == 0 else (col_4 - row_4)
        tri = jnp.where(ahead >= 0, 1.0, 0.0).astype(BF16)
        pair_masks = []
        for sh in range(int(math.log2(c_len))):
            same_pair = (row_c >> (sh + 1)) == (col_c >> (sh + 1))
            pair_masks.append(same_pair & ((row_c >> sh) != (col_c >> sh)) & (ahead_c > 0))
        masks.append(dict(pair=pair_masks, strict=ahead_c > 0, incl4=ahead_4 >= 0, tri=tri))

    def scaled_operands(d, g):
        r_ref, lw_ref, k_ref, v_ref, kk_ref, b_ref = ins[d]
        rows = pl.ds(g * c_len, c_len)
        tri = masks[d]['tri']
        lw = lw_ref[rows, :]
        h1, h2, h3 = _split3(lw)
        c = _dot(tri, h1) + _dot(tri, h2) + _dot(tri, h3)
        tot = jnp.sum(lw, axis=0, keepdims=True)
        e_neg = jnp.exp(-c)
        dec = jnp.exp(tot - c)
        f32 = lambda ref: ref[rows, :].astype(F32)
        b_f, k_f = f32(b_ref), f32(k_ref)
        return dict(
            a_t=-f32(kk_ref) * jnp.exp(c - lw), r_t=f32(r_ref) * jnp.exp(c),
            b_t=(b_f * e_neg).astype(BF16), k_t=(k_f * e_neg).astype(BF16),
            b_p=(b_f * dec).astype(BF16), k_p=(k_f * dec).astype(BF16),
            p_tot=jnp.exp(tot), v=v_ref[rows, :])

    lanes_of = lambda p: slice(p * LANES, (p + 1) * LANES)
    sub_of = lambda d, step: step if d == 0 else n_sub - 1 - step
    state = {(d, p): s_ref[d, p] for d in range(2) for p in range(n_pairs)}
    wave_len = min(SCAN_WAVE, n_sub)

    n_waves = n_sub // wave_len
    wave_subs = lambda wave: [(d, sub_of(d, s)) for s in range(wave * wave_len, (wave + 1) * wave_len)
                              for d in range(2)]
    next_ops = {dg: scaled_operands(*dg) for dg in wave_subs(0)}
    for wave in range(n_waves):
        steps = range(wave * wave_len, (wave + 1) * wave_len)
        ops, next_ops = next_ops, {}
        pending = wave_subs(wave + 1) if wave + 1 < n_waves else []
        groups = [(d, g, p) for (d, g) in ops for p in range(n_pairs)]

        scores, v_bd = {}, {}
        for gr in groups:
            d, g, p = gr
            sl = lanes_of(p)
            lhs = jnp.concatenate([ops[d, g]['a_t'][:, sl], ops[d, g]['r_t'][:, sl]], axis=0).astype(BF16)
            rhs = jnp.concatenate([by_head(ops[d, g]['b_t'][:, sl]), by_head(ops[d, g]['k_t'][:, sl])], axis=0)
            scores[gr] = _dot_nt(lhs, rhs)
            v_bd[gr] = by_head(ops[d, g]['v'][:, sl])
        w0, l_mat, x_inv = {}, {}, {}
        for gr in groups:
            d = gr[0]
            top = scores[gr][:c_len]
            l_mat[gr] = top[:, :LANES]
            w0[gr] = _dot(jnp.where(masks[d]['strict'], top[:, LANES:], 0.0).astype(BF16), v_bd[gr])
            x_inv[gr] = eye_c + jnp.where(masks[d]['pair'][0], l_mat[gr], 0.0)

        for level in range(1, int(math.log2(c_len))):
            t1 = {}
            for gr in groups:
                pm = masks[gr[0]]['pair'][level]
                t1[gr] = _dot(jnp.where(pm, l_mat[gr], 0.0).astype(BF16), by_head(x_inv[gr].astype(BF16)))
            for gr in groups:
                x_inv[gr] = x_inv[gr] + _dot(x_inv[gr].astype(BF16), by_head(t1[gr].astype(BF16)))
            if pending:
                dg = pending.pop(0)
                next_ops[dg] = scaled_operands(*dg)
        for dg in pending:
            next_ops[dg] = scaled_operands(*dg)
        xa_u0 = {}
        for gr in groups:
            d, g, p = gr
            rhs = jnp.concatenate([by_head(ops[d, g]['a_t'][:, lanes_of(p)].astype(BF16)),
                                   by_head(w0[gr].astype(BF16))], axis=1)
            xa_u0[gr] = _dot(x_inv[gr].astype(BF16), rhs)
        p_row_of = {dg: i for i, dg in enumerate(ops)}
        p_all = jnp.concatenate([ops[dg]['p_tot'] for dg in ops]
                                + [jnp.zeros((LANES - len(ops), RWKV_DIM), F32)], axis=0)
        p_t = [jnp.transpose(p_all[:, lanes_of(p)]) for p in range(n_pairs)]
        bk_t = {}
        for d, g, p in groups:
            sl = lanes_of(p)
            bk_t[d, g, p] = jnp.transpose(
                jnp.concatenate([ops[d, g]['b_p'][:, sl], ops[d, g]['k_p'][:, sl]], axis=0))
        lhs_t, p_col = {}, {}
        for gr in groups:
            d, g, p = gr
            lhs_t[gr] = jnp.concatenate([xa_u0[gr][:, :LANES].astype(BF16),
                                         ops[d, g]['r_t'][:, lanes_of(p)].astype(BF16)], axis=0)
            r_idx = p_row_of[d, g]
            p_col[gr] = jnp.broadcast_to(p_t[p][:, r_idx:r_idx + 1], (LANES, LANES))

        for step in steps:
            live = [(d, sub_of(d, step), p) for d in range(2) for p in range(n_pairs)]
            prod = {gr: _dot(lhs_t[gr], state[gr[0], gr[2]].astype(BF16)) for gr in live}
            u = {gr: (prod[gr][:c_len] + xa_u0[gr][:, LANES:]).astype(BF16) for gr in live}
            upd = {gr: _dot(bk_t[gr],
                            jnp.concatenate([u[gr], ops[gr[0], gr[1]]['v'][:, lanes_of(gr[2])]], axis=0))
                   for gr in live}
            for gr in live:
                d, g, p = gr
                state[d, p] = p_col[gr] * state[d, p] + jnp.where(same_head, upd[gr], 0.0)
            for gr in live:
                d, g, p = gr
                y = _dot(jnp.where(masks[d]['incl4'], scores[gr][c_len:], 0.0).astype(BF16),
                         jnp.concatenate([by_head(u[gr]), v_bd[gr]], axis=0))
                y_refs[d][pl.ds(g * c_len, c_len), lanes_of(p)] = prod[gr][c_len:] + y
    for (d, p), s_new in state.items():
        s_ref[d, p] = s_new


def _rwkv_scan(r, lw, kd, v, kk, bvec, batch, seq, ctx_len):
    m, hd = r.shape
    n_sub = _pick_tile(SCAN_SUBCHUNKS, ctx_len // SCAN_CHUNK, seq // SCAN_CHUNK)
    c_len = n_sub * SCAN_CHUNK
    n_ctx, n_lat = ctx_len // c_len, seq // c_len
    ctx0 = batch * seq // c_len

    def rowblk(d):
        def fn(b, j):
            jc = j if d == 0 else n_ctx - 1 - j
            jl = (j - n_ctx) if d == 0 else n_lat - 1 - (j - n_ctx)
            return jnp.where(j < n_ctx, ctx0 + b * n_ctx + jc, b * n_lat + jl)
        return fn

    in_specs, args = [], []
    for d in range(2):
        shared = pl.BlockSpec((c_len, hd), lambda b, j, f=rowblk(d): (f(b, j), 0))
        per_dir = pl.BlockSpec((c_len, hd), lambda b, j, f=rowblk(d), d=d: (f(b, j), d))
        in_specs += [shared, per_dir, per_dir, shared, shared, per_dir]
        args += [r, lw, kd, v, kk, bvec]
    out = jax.ShapeDtypeStruct((m, hd), F32)
    return pl.pallas_call(
        _scan_kernel,
        grid=(batch, n_ctx + n_lat),
        in_specs=in_specs,
        out_specs=[pl.BlockSpec((c_len, hd), lambda b, j, f=rowblk(d): (f(b, j), 0)) for d in range(2)],
        out_shape=[out, out],
        scratch_shapes=[pltpu.VMEM((2, hd // LANES, LANES, LANES), F32)],
        compiler_params=_cparams("parallel", "arbitrary"),
        name="rwkv_scan",
    )(*args)


def _merge_kernel(*refs, n_x, n_ya, n_lat_tiles):
    x_refs, ya_refs = refs[:n_x], refs[n_x:n_x + n_ya]
    (yc_ref, yf_ref, yb_ref, bonus_ref, rg_ref, gg_ref, gb_ref, seg_ref,
     gt_ref, pc_ref, pa_ref, pr_ref, wo_ref, gate_ref, o_ref) = refs[n_x + n_ya:]
    d = o_ref.shape[1]
    seg = seg_ref[...]
    ys = yf_ref[...] + yb_ref[...]
    inv_n = 1.0 / RWKV_HEAD_DIM
    mu = _seg_sum(ys, seg) * inv_n
    dlt = ys - mu
    var = _seg_sum(dlt * dlt, seg) * inv_n
    yn = dlt * lax.rsqrt(var + GN_EPS) * gg_ref[...] + gb_ref[...]
    y_rw = ((yn + bonus_ref[...].astype(F32)) * rg_ref[...].astype(F32)).astype(BF16)
    g = jax.nn.sigmoid(gt_ref[...].astype(F32))
    mix = (g[:, 0:d] * _dot(yc_ref[...], pc_ref[...])
           + g[:, d:2 * d] * _dot(_tile_rows(ya_refs, n_lat_tiles), pa_ref[...])
           + g[:, 2 * d:] * _dot(y_rw, pr_ref[...]))
    o_ref[...] = _tile_rows(x_refs, n_lat_tiles) + gate_ref[...] * _dot(mix.astype(BF16), wo_ref[...])


def _merge(x_parts, rows, yc, ya_parts, yf, yb, bonus, rg, gn_g, gn_b, gates, pc, pa, pr, wo, mod_tab, tm,
           row_of_tile):
    d = x_parts[0].shape[1]
    hd = yf.shape[1]
    seg = _head_seg_matrix(min(hd, MXU_WIDTH), RWKV_HEAD_DIM)
    gn_g, gn_b = gn_g.reshape(1, hd), gn_b.reshape(1, hd)
    tok = lambda a: pl.BlockSpec((tm, a.shape[1]), lambda i: (i, 0))
    full = lambda a: pl.BlockSpec(a.shape, lambda i: (0, 0))
    x_specs, x_args = _row_specs(x_parts, tm)
    ya_specs, ya_args = _row_specs(ya_parts, tm)
    return pl.pallas_call(
        functools.partial(_merge_kernel, n_x=len(x_args), n_ya=len(ya_args),
                          n_lat_tiles=ya_parts[0].shape[0] // tm),
        grid=(rows // tm,),
        in_specs=x_specs + ya_specs + [
            tok(yc), tok(yf), tok(yb), tok(bonus), tok(rg), full(gn_g), full(gn_b),
            full(seg), tok(gates), full(pc), full(pa), full(pr), full(wo),
            pl.BlockSpec((None, None, 1, d), lambda i: (row_of_tile(i), 2, 0, 0))],
        out_specs=pl.BlockSpec((tm, d), lambda i: (i, 0)),
        out_shape=jax.ShapeDtypeStruct((rows, d), F32),
        compiler_params=_cparams("parallel"),
        name="merge_out_proj",
    )(*x_args, *ya_args, yc, yf, yb, bonus, rg, gn_g, gn_b, seg, gates, pc, pa, pr, wo, mod_tab)


def _mlp_kernel(*refs, final):
    if final:
        x_ref, g_ref, sh_ref, sc_ref, gate_ref, w1_ref, w2_ref, fg_ref, o_ref, h_ref, acc_ref = refs
    else:
        x_ref, g_ref, sh_ref, sc_ref, gate_ref, w1_ref, w2_ref, o_ref, h_ref, acc_ref = refs
    f = pl.program_id(1)

    @pl.when(f == 0)
    def _():
        h_ref[...] = _modulated_norm(x_ref[...], g_ref[...], sh_ref[...], sc_ref[...]).astype(BF16)
        acc_ref[...] = jnp.zeros_like(acc_ref)

    t = jnp.maximum(_dot(h_ref[...], w1_ref[...]), 0.0)
    acc_ref[...] += _dot((t * t).astype(BF16), w2_ref[...])

    @pl.when(f == pl.num_programs(1) - 1)
    def _():
        y = x_ref[...] + gate_ref[...] * acc_ref[...]
        if final:
            y = y * lax.rsqrt(jnp.mean(y * y, axis=-1, keepdims=True) + NORM_EPS) * fg_ref[...]
        o_ref[...] = y


def _mlp(x, rows, gain, mod_tab, w1, w2, final_g, tm, tf, row_of_tile):
    d = x.shape[1]
    dff = w1.shape[1]
    final = final_g is not None
    tab = lambda which: pl.BlockSpec((None, None, 1, d), lambda i, f: (row_of_tile(i), which, 0, 0))
    in_specs = [pl.BlockSpec((tm, d), lambda i, f: (i, 0)),
                pl.BlockSpec((1, d), lambda i, f: (0, 0)),
                tab(3), tab(4), tab(5),
                pl.BlockSpec((d, tf), lambda i, f: (0, f)),
                pl.BlockSpec((tf, d), lambda i, f: (f, 0))]
    args = [x, gain.reshape(1, d), mod_tab, mod_tab, mod_tab, w1, w2]
    if final:
        in_specs.append(pl.BlockSpec((1, d), lambda i, f: (0, 0)))
        args.append(final_g.reshape(1, d))
    return pl.pallas_call(
        functools.partial(_mlp_kernel, final=final),
        grid=(rows // tm, dff // tf),
        in_specs=in_specs,
        out_specs=pl.BlockSpec((tm, d), lambda i, f: (i, 0)),
        out_shape=jax.ShapeDtypeStruct((rows, d), F32),
        scratch_shapes=[pltpu.VMEM((tm, d), BF16), pltpu.VMEM((tm, d), F32)],
        compiler_params=_cparams("parallel", "arbitrary"),
        name="mlp_final" if final else "mlp",
    )(*args)


def kernel(x, c, ctx, c_ctx, mod_w, mod_b, norm1_g, norm2_g, w_in, conv_dw_w, conv_dw_b, conv_ln_g, conv_ln_b, p_conv, att_lq1, att_lk1, att_lq2, att_lk2, att_subln_g, p_att, rwkv_shift, rwkv_w0, rwkv_w2, rwkv_a0, rwkv_a2, rwkv_g2, rwkv_kk, rwkv_ka, rwkv_rk, rwkv_gn_g, rwkv_gn_b, p_rwkv, w_out, mlp_w1, mlp_w2, final_g):
    batch, seq, d = x.shape
    ctx_len = ctx.shape[1]
    depth = mod_w.shape[0]
    n_lat = batch * seq
    conv_w = 2 * conv_dw_w.shape[2]
    att_w = 3 * ATT_HEADS * ATT_VAL_DIM
    rw_w = 3 * RWKV_DIM + 2 * DECAY_LORA + 2 * ICLR_LORA + GATE_LORA
    assert w_in.shape[2] == conv_w + att_w + rw_w + 3 * d
    assert seq % SCAN_CHUNK == 0 and ctx_len % SCAN_CHUNK == 0 and (batch * seq) % ctx_len == 0

    tm = _pick_tile(1024, seq, batch * ctx_len)
    tmm = _pick_tile(512, seq, batch * ctx_len)
    tc = _pick_tile(256, seq, ctx_len)
    tiles_per_seq = seq // tm
    row_of_tile = lambda i: jnp.minimum(i // tiles_per_seq, batch)
    row_of_tile_m = lambda i: jnp.minimum(i // (seq // tmm), batch)

    xs = (x.reshape(n_lat, d), ctx.reshape(batch * ctx_len, d))
    n_rows = n_lat + batch * ctx_len
    pad = (-(batch + 1)) % 8
    c_all = jnp.concatenate([c, c_ctx[None, :], jnp.zeros((pad, d), F32)], axis=0)
    rope_tabs = _rope_tables(seq)

    for l in range(depth):
        last = l == depth - 1
        lambda_init = 0.8 - 0.6 * math.exp(-0.3 * l)
        mod = _modulation(c_all, mod_w[l], mod_b[l])
        mod_tab = mod[:batch + 1].reshape(batch + 1, 6, 1, d)

        wl = w_in[l].astype(BF16)
        o1, o2, o3 = conv_w, conv_w + att_w, conv_w + att_w + rw_w
        zc, h1 = _norm_project(xs, norm1_g[l], mod_tab, 0, wl[:, :o1], tm, row_of_tile, "norm_proj_conv")
        qkv = _project(h1, wl[:, o1:o2], tm, "proj_qkv")
        zr = _project(h1, wl[:, o2:o3], tm, "proj_rwkv")
        gates = _project(h1, wl[:, o3:], tm, "proj_gates")

        y_conv = _conv_branch(zc, conv_dw_w[l], conv_dw_b[l], conv_ln_g[l], conv_ln_b[l],
                              tc, n_lat, seq, ctx_len)

        lam_vecs = jnp.stack([att_lq1[l], att_lk1[l], att_lq2[l], att_lk2[l]])
        y_att = (_attention(qkv, rope_tabs, lam_vecs, att_subln_g[l], lambda_init, batch, seq, ctx_len, True),
                 None if last else
                 _attention(qkv, None, lam_vecs, att_subln_g[l], lambda_init, batch, seq, ctx_len, False))

        lp = dict(rwkv_shift=rwkv_shift[l], rwkv_w0=rwkv_w0[l], rwkv_w2=rwkv_w2[l], rwkv_a0=rwkv_a0[l],
                  rwkv_a2=rwkv_a2[l], rwkv_g2=rwkv_g2[l], rwkv_kk=rwkv_kk[l], rwkv_ka=rwkv_ka[l],
                  rwkv_rk=rwkv_rk[l])
        r, v, kk, g, bonus, lw, kd, bvec = _rwkv_prep(zr, lp, tc, n_lat, seq, ctx_len)
        yf, yb = _rwkv_scan(r, lw, kd, v, kk, bvec, batch, seq, ctx_len)

        rows = n_lat if last else n_rows
        xm = _merge(xs, rows, y_conv, y_att, yf, yb, bonus, g, rwkv_gn_g[l], rwkv_gn_b[l], gates,
                    p_conv[l].astype(BF16), p_att[l].astype(BF16),
                    p_rwkv[l].astype(BF16), w_out[l].astype(BF16), mod_tab, tmm, row_of_tile_m)
        xs = (_mlp(xm, rows, norm2_g[l], mod_tab, mlp_w1[l].astype(BF16), mlp_w2[l].astype(BF16),
                   final_g if last else None, tm, _pick_tile(MLP_FF_TILE, mlp_w1.shape[2]), row_of_tile), None)
    return xs[0].reshape(batch, seq, d)
```

```python
import functools
import math

import jax
import jax.numpy as jnp
from jax import lax
from jax.experimental import pallas as pl
from jax.experimental.pallas import tpu as pltpu

F32 = jnp.float32
BF16 = jnp.bfloat16

GRID_W = 64
NORM_EPS = 1e-6
CONV_WIDTH = 31
LN_EPS = 1e-5
ATT_HEADS = 4
ATT_HEAD_DIM = 64
ATT_VAL_DIM = 2 * ATT_HEAD_DIM
ROPE_BASE = 10000.0
SUBLN_EPS = 1e-5
RWKV_HEADS = 8
RWKV_HEAD_DIM = 64
RWKV_DIM = RWKV_HEADS * RWKV_HEAD_DIM
DECAY_LORA = 64
ICLR_LORA = 64
GATE_LORA = 128
GN_EPS = 64e-5

LANES = 128
SUBLANES = 8
MXU_WIDTH = 256
HALO = 16
SCAN_CHUNK = 64
SCAN_SUBCHUNKS = 4
SCAN_WAVE = 2
MLP_FF_TILE = 1024
PROJ_ROW_PARTS = 2
ATT_Q_TILE = 2048
ATT_ROW_PARTS = 16
VMEM_LIMIT = 56 << 20


def _cparams(*sem):
    return pltpu.CompilerParams(dimension_semantics=sem, vmem_limit_bytes=VMEM_LIMIT)


def _pick_tile(limit, *sizes):
    t = limit
    while any(s % t for s in sizes):
        t //= 2
    return t


def _dot(a, b):
    return jnp.dot(a, b, preferred_element_type=F32)


def _dot_nt(a, b):
    return lax.dot_general(a, b, (((1,), (1,)), ((), ())), preferred_element_type=F32)


def _dot_tn(a, b):
    return lax.dot_general(a, b, (((0,), (0,)), ((), ())), preferred_element_type=F32)


def _seg_sum(x, seg):
    w = seg.shape[0]
    hi = x.astype(BF16)
    lo = (x - hi.astype(F32)).astype(BF16)
    return jnp.concatenate([_dot(hi[:, j:j + w], seg) + _dot(lo[:, j:j + w], seg)
                            for j in range(0, x.shape[1], w)], axis=1)


def _modulated_norm(x, gain, shift, scale):
    y = x * lax.rsqrt(jnp.mean(x * x, axis=-1, keepdims=True) + NORM_EPS)
    return (y * gain) * (1.0 + scale) + shift


def _mod_kernel(c_ref, w_ref, b_ref, o_ref):
    c = c_ref[...]
    s = c * jax.nn.sigmoid(c)
    o_ref[...] = _dot(s.astype(BF16), w_ref[...].astype(BF16)) + b_ref[...]


def _modulation(c_all, mod_w, mod_b):
    rows, d = c_all.shape
    n = mod_w.shape[1]
    tn = _pick_tile(1024, n // 6)
    return pl.pallas_call(
        _mod_kernel,
        grid=(n // tn,),
        in_specs=[pl.BlockSpec((rows, d), lambda j: (0, 0)),
                  pl.BlockSpec((d, tn), lambda j: (0, j)),
                  pl.BlockSpec((1, tn), lambda j: (0, j))],
        out_specs=pl.BlockSpec((rows, tn), lambda j: (0, j)),
        out_shape=jax.ShapeDtypeStruct((rows, n), F32),
        compiler_params=_cparams("arbitrary"),
        name="adaln_modulation",
    )(c_all, mod_w, mod_b.reshape(1, n))


def _row_specs(parts, tile):
    lat, ctx_rows = parts
    cols = lat.shape[1]
    if ctx_rows is None:
        return [pl.BlockSpec((tile, cols), lambda i: (i, 0))], [lat]
    n_lat_tiles = lat.shape[0] // tile
    return ([pl.BlockSpec((tile, cols), lambda i: (jnp.minimum(i, n_lat_tiles - 1), 0)),
             pl.BlockSpec((tile, cols), lambda i: (jnp.maximum(i - n_lat_tiles, 0), 0))], [lat, ctx_rows])


def _tile_rows(refs, n_lat_tiles, rows=slice(None)):
    if len(refs) == 1:
        return refs[0][rows, :]
    return jnp.where(pl.program_id(0) < n_lat_tiles, refs[0][rows, :], refs[1][rows, :])


def _norm_proj_kernel(*refs, n_x, n_lat_tiles):
    x_refs = refs[:n_x]
    g_ref, sh_ref, sc_ref, w_ref, o_ref, h_ref = refs[n_x:]
    part = o_ref.shape[0] // PROJ_ROW_PARTS
    for i in range(PROJ_ROW_PARTS):
        rows = pl.ds(i * part, part)
        x = _tile_rows(x_refs, n_lat_tiles, rows)
        h = _modulated_norm(x, g_ref[...], sh_ref[...], sc_ref[...]).astype(BF16)
        h_ref[rows, :] = h
        o_ref[rows, :] = _dot(h, w_ref[...]).astype(o_ref.dtype)


def _norm_project(x_parts, gain, mod_tab, which_shift, w, tm, row_of_tile, name):
    m = sum(a.shape[0] for a in x_parts if a is not None)
    d = x_parts[0].shape[1]
    n = w.shape[1]
    x_specs, x_args = _row_specs(x_parts, tm)
    tab = lambda which: pl.BlockSpec((None, None, 1, d), lambda i: (row_of_tile(i), which, 0, 0))
    return pl.pallas_call(
        functools.partial(_norm_proj_kernel, n_x=len(x_args), n_lat_tiles=x_parts[0].shape[0] // tm),
        grid=(m // tm,),
        in_specs=x_specs + [pl.BlockSpec((1, d), lambda i: (0, 0)),
                            tab(which_shift), tab(which_shift + 1),
                            pl.BlockSpec((d, n), lambda i: (0, 0))],
        out_specs=[pl.BlockSpec((tm, n), lambda i: (i, 0)), pl.BlockSpec((tm, d), lambda i: (i, 0))],
        out_shape=[jax.ShapeDtypeStruct((m, n), BF16), jax.ShapeDtypeStruct((m, d), BF16)],
        compiler_params=_cparams("parallel"),
        name=name,
    )(*x_args, gain.reshape(1, d), mod_tab, mod_tab, w)


def _proj_kernel(h_ref, w_ref, o_ref):
    o_ref[...] = _dot(h_ref[...], w_ref[...]).astype(o_ref.dtype)


def _project(h, w, tm, name):
    m, d = h.shape
    n = w.shape[1]
    return pl.pallas_call(
        _proj_kernel,
        grid=(m // tm,),
        in_specs=[pl.BlockSpec((tm, d), lambda i: (i, 0)), pl.BlockSpec((d, n), lambda i: (0, 0))],
        out_specs=pl.BlockSpec((tm, n), lambda i: (i, 0)),
        out_shape=jax.ShapeDtypeStruct((m, n), BF16),
        compiler_params=_cparams("parallel"),
        name=name,
    )(h, w)


def _seq_edges(i, tile, n_lat_rows, seq, ctx_len):
    n_lat_tiles = n_lat_rows // tile
    is_lat = i < n_lat_tiles
    per = jnp.where(is_lat, seq // tile, ctx_len // tile)
    pos = jnp.where(is_lat, i, i - n_lat_tiles) % per
    return pos == 0, pos == per - 1


def _conv_kernel(z_ref, zp_ref, zn_ref, w_ref, b_ref, lg_ref, lb_ref, o_ref, ext_ref, rot_ref,
                 *, tile, n_lat_rows, seq, ctx_len):
    first, last = _seq_edges(pl.program_id(0), tile, n_lat_rows, seq, ctx_len)
    cd = o_ref.shape[1]

    def glu(z):
        z = z.astype(F32)
        return z[:, :cd] * jax.nn.sigmoid(z[:, cd:])

    ext_ref[0:HALO, :] = jnp.where(first, 0.0, glu(zp_ref[...]))
    ext_ref[HALO:HALO + tile, :] = glu(z_ref[...])
    ext_ref[HALO + tile:, :] = jnp.where(last, 0.0, glu(zn_ref[...]))
    span = rot_ref.shape[1]
    for r in range(SUBLANES):
        rot_ref[r] = ext_ref[pl.ds(r, span), :]
    half = CONV_WIDTH // 2
    acc = jnp.zeros((tile, cd), F32)
    for k in range(CONV_WIDTH):
        start = HALO - half + k
        acc = acc + w_ref[k:k + 1, :] * rot_ref[start % SUBLANES, pl.ds(start - start % SUBLANES, tile), :]
    u = acc + b_ref[...]
    mu = jnp.mean(u, axis=-1, keepdims=True)
    dlt = u - mu
    var = jnp.mean(dlt * dlt, axis=-1, keepdims=True)
    y = dlt * lax.rsqrt(var + LN_EPS) * lg_ref[...] + lb_ref[...]
    o_ref[...] = (y * jax.nn.sigmoid(y)).astype(o_ref.dtype)


def _halo_specs(tile, width, m):
    per = tile // HALO
    last_blk = m // HALO - 1
    return [pl.BlockSpec((tile, width), lambda i: (i, 0)),
            pl.BlockSpec((HALO, width), lambda i: (jnp.maximum(i * per - 1, 0), 0)),
            pl.BlockSpec((HALO, width), lambda i: (jnp.minimum((i + 1) * per, last_blk), 0))]


def _conv_branch(zc, dw_w, dw_b, ln_g, ln_b, tile, n_lat_rows, seq, ctx_len):
    m, width = zc.shape
    cd = width // 2
    vec = pl.BlockSpec((1, cd), lambda i: (0, 0))
    return pl.pallas_call(
        functools.partial(_conv_kernel, tile=tile, n_lat_rows=n_lat_rows, seq=seq, ctx_len=ctx_len),
        grid=(m // tile,),
        in_specs=_halo_specs(tile, width, m) + [pl.BlockSpec((CONV_WIDTH, cd), lambda i: (0, 0)),
                                                vec, vec, vec],
        out_specs=pl.BlockSpec((tile, cd), lambda i: (i, 0)),
        out_shape=jax.ShapeDtypeStruct((m, cd), BF16),
        scratch_shapes=[pltpu.VMEM((tile + 2 * HALO, cd), F32),
                        pltpu.VMEM((SUBLANES, tile + 2 * HALO - SUBLANES, cd), F32)],
        compiler_params=_cparams("parallel"),
        name="conv_branch",
    )(zc, zc, zc, dw_w, dw_b.reshape(1, cd), ln_g.reshape(1, cd), ln_b.reshape(1, cd))


def _rope(x, cos, sin_next, sin_prev):
    return x * cos + pltpu.roll(x, LANES - 1, 1) * sin_next + pltpu.roll(x, 1, 1) * sin_prev


def _attn_kernel(*refs, lambda_init, n_ctx, with_lat):
    if with_lat:
        (q_ref, kc_ref, vc_ref, kl_ref, vl_ref, cq_ref, snq_ref, spq_ref, ck_ref, snk_ref, spk_ref,
         lam_ref, g_ref, o_ref, ks_ref, vs_ref) = refs
    else:
        q_ref, kc_ref, vc_ref, lam_ref, g_ref, o_ref, ks_ref, vs_ref = refs
    dv = ATT_VAL_DIM

    @pl.when(pl.program_id(2) == 0)
    def _():
        vs_ref[:, dv:] = jnp.ones((vs_ref.shape[0], dv), BF16)
        ks_ref[0:n_ctx, :] = kc_ref[...]
        vs_ref[0:n_ctx, 0:dv] = vc_ref[...]
        if with_lat:
            k = _rope(kl_ref[...].astype(F32), ck_ref[...], snk_ref[...], spk_ref[...])
            ks_ref[n_ctx:, :] = k.astype(BF16)
            vs_ref[n_ctx:, 0:dv] = vl_ref[...]

    tq = q_ref.shape[0]
    q = q_ref[...].astype(F32)
    if with_lat:
        q = _rope(q, cq_ref[...], snq_ref[...], spq_ref[...])
    q = q * (math.log2(math.e) / math.sqrt(ATT_HEAD_DIM))
    first_map = lax.broadcasted_iota(jnp.int32, q.shape, 1) < ATT_HEAD_DIM
    qs = jnp.concatenate([jnp.where(first_map, q, 0.0), jnp.where(first_map, 0.0, q)], axis=0).astype(BF16)
    part = (2 * tq) // ATT_ROW_PARTS
    scores = lambda i: _dot_nt(qs[i * part:(i + 1) * part], ks_ref[...])
    s_next = scores(0)
    outs = []
    for i in range(ATT_ROW_PARTS):
        s = s_next
        if i + 1 < ATT_ROW_PARTS:
            s_next = scores(i + 1)
        p = jnp.exp2((s - jnp.max(s, axis=-1, keepdims=True)).astype(BF16))
        outs.append(_dot(p, vs_ref[...]))
    ol = jnp.concatenate(outs, axis=0)
    o = ol[:, 0:dv] / ol[:, dv:]
    lq = lam_ref[...]
    lam = (jnp.exp(jnp.sum(lq[0:1] * lq[1:2], axis=-1, keepdims=True))
           - jnp.exp(jnp.sum(lq[2:3] * lq[3:4], axis=-1, keepdims=True)) + lambda_init)
    a = o[:tq] - lam * o[tq:]
    y = a * lax.rsqrt(jnp.mean(a * a, axis=-1, keepdims=True) + SUBLN_EPS) * g_ref[...]
    o_ref[...] = (y * (1.0 - lambda_init)).astype(o_ref.dtype)


def _attention(qkv, rope_tabs, lam_vecs, subln_g, lambda_init, batch, seq, ctx_len, latent_queries):
    h_ = ATT_HEADS
    ctx_blk0 = batch * seq // ctx_len
    tq = _pick_tile(ATT_Q_TILE, seq if latent_queries else ctx_len)
    nq = (seq if latent_queries else ctx_len) // tq
    q0 = 0 if latent_queries else batch * seq // tq
    blk = lambda rows, fn: pl.BlockSpec((rows, LANES), fn)
    in_specs = [blk(tq, lambda b, h, i: (q0 + b * nq + i, h)),
                blk(ctx_len, lambda b, h, i: (ctx_blk0 + b, h_ + h)),
                blk(ctx_len, lambda b, h, i: (ctx_blk0 + b, 2 * h_ + h))]
    args = [qkv, qkv, qkv]
    n_keys = ctx_len
    if latent_queries:
        n_keys += seq
        in_specs += [blk(seq, lambda b, h, i: (b, h_ + h)), blk(seq, lambda b, h, i: (b, 2 * h_ + h))]
        in_specs += [blk(tq, lambda b, h, i: (i, 0))] * 3 + [blk(seq, lambda b, h, i: (0, 0))] * 3
        args += [qkv, qkv] + list(rope_tabs) + list(rope_tabs)
    in_specs += [pl.BlockSpec((4, ATT_HEAD_DIM), lambda b, h, i: (0, 0)),
                 pl.BlockSpec((1, ATT_VAL_DIM), lambda b, h, i: (0, 0))]
    args += [lam_vecs, subln_g.reshape(1, ATT_VAL_DIM)]
    out_rows = batch * (seq if latent_queries else ctx_len)
    return pl.pallas_call(
        functools.partial(_attn_kernel, lambda_init=lambda_init, n_ctx=ctx_len, with_lat=latent_queries),
        grid=(batch, h_, nq),
        in_specs=in_specs,
        out_specs=blk(tq, lambda b, h, i: (b * nq + i, h)),
        out_shape=jax.ShapeDtypeStruct((out_rows, h_ * ATT_VAL_DIM), BF16),
        scratch_shapes=[pltpu.VMEM((n_keys, LANES), BF16), pltpu.VMEM((n_keys, 2 * ATT_VAL_DIM), BF16)],
        compiler_params=_cparams("parallel", "parallel", "arbitrary"),
        name="diff_attention_latent" if latent_queries else "diff_attention_context",
    )(*args)


def _rope_tables(seq):
    t = jnp.arange(seq, dtype=jnp.int32)
    row = (t // GRID_W).astype(F32)
    col = (t % GRID_W).astype(F32)
    n_pairs = ATT_HEAD_DIM // 4
    inv_freq = ROPE_BASE ** (-jnp.arange(n_pairs, dtype=F32) / n_pairs)
    ang = jnp.concatenate([row[:, None] * inv_freq, col[:, None] * inv_freq], axis=-1)
    cos = jnp.repeat(jnp.cos(ang), 2, axis=-1)
    sin = jnp.repeat(jnp.sin(ang), 2, axis=-1)
    even = (jnp.arange(ATT_HEAD_DIM) % 2 == 0)[None, :]
    both = lambda a: jnp.concatenate([a, a], axis=-1)
    return both(cos), both(jnp.where(even, -sin, 0.0)), both(jnp.where(even, 0.0, sin))


def _rwkv_prep_kernel(z_ref, zp_ref, zn_ref, smat_ref, sh_ref, w2_ref, a2_ref, g2_ref, w0_ref, a0_ref,
                      kkw_ref, ka_ref, rk_ref, seg_ref,
                      r_o, v_o, kk_o, g_o, bonus_o, lw_o, kd_o, b_o,
                      *, tile, n_lat_rows, seq, ctx_len):
    first, last = _seq_edges(pl.program_id(0), tile, n_lat_rows, seq, ctx_len)
    hd = RWKV_DIM
    zc = z_ref[...]
    shifted = _dot(smat_ref[...], zc)
    prev_row = jnp.where(first, 0.0, zp_ref[...].astype(F32)[HALO - 1:HALO, :])
    next_row = jnp.where(last, 0.0, zn_ref[...].astype(F32)[0:1, :])
    t_idx = lax.broadcasted_iota(jnp.int32, (tile, 1), 0)
    z = (sh_ref[0:1, :] * jnp.where(t_idx == 0, prev_row, shifted[:tile])
         + sh_ref[1:2, :] * zc.astype(F32)
         + sh_ref[2:3, :] * jnp.where(t_idx == tile - 1, next_row, shifted[tile:]))
    seg = seg_ref[...]
    r = z[:, 0:hd]
    k = z[:, hd:2 * hd]
    v = z[:, 2 * hd:3 * hd]
    o = 3 * hd
    wd = z[:, o:o + 2 * DECAY_LORA]
    ad = z[:, o + 2 * DECAY_LORA:o + 2 * DECAY_LORA + 2 * ICLR_LORA]
    gd = z[:, o + 2 * DECAY_LORA + 2 * ICLR_LORA:]
    kkf = k * kkw_ref[...]
    kk = kkf * lax.rsqrt(jnp.maximum(_seg_sum(kkf * kkf, seg), 1e-24))
    g = _dot(jax.nn.sigmoid(gd).astype(BF16), g2_ref[...])
    wl = w0_ref[...] + _dot(jnp.tanh(wd).astype(BF16), w2_ref[...])
    al = a0_ref[...] + _dot(ad.astype(BF16), a2_ref[...])
    softplus = jnp.maximum(-wl, 0.0) + jnp.log(1.0 + jnp.exp(-jnp.abs(wl)))
    lw = -jnp.exp(-softplus - 0.5)
    a = jax.nn.sigmoid(al)
    r_o[...] = r.astype(r_o.dtype)
    v_o[...] = v.astype(v_o.dtype)
    kk_o[...] = kk.astype(kk_o.dtype)
    g_o[...] = g.astype(g_o.dtype)
    lw_o[...] = lw
    kd_sum = jnp.zeros_like(r)
    for d in range(2):
        a_d = a[:, d * hd:(d + 1) * hd]
        kd = k * (1.0 + (a_d - 1.0) * ka_ref[...])
        kd_o[:, d * hd:(d + 1) * hd] = kd.astype(kd_o.dtype)
        b_o[:, d * hd:(d + 1) * hd] = (kk * a_d).astype(b_o.dtype)
        kd_sum = kd_sum + kd
    bonus_o[...] = (_seg_sum(r * kd_sum * rk_ref[...], seg) * v).astype(bonus_o.dtype)


def _block_diag2(w):
    z = jnp.zeros_like(w[0])
    return jnp.concatenate([jnp.concatenate([w[0], z], axis=1), jnp.concatenate([z, w[1]], axis=1)], axis=0)


def _head_seg_matrix(n, head):
    idx = jnp.arange(n) // head
    return (idx[:, None] == idx[None, :]).astype(BF16)


def _rwkv_prep(zr, lp, tile, n_lat_rows, seq, ctx_len):
    m, width = zr.shape
    hd = RWKV_DIM
    full = lambda a: pl.BlockSpec(a.shape, lambda i: (0,) * a.ndim)
    shift_mat = jnp.concatenate([jnp.eye(tile, k=-1, dtype=BF16), jnp.eye(tile, k=1, dtype=BF16)], axis=0)
    consts = [shift_mat, lp['rwkv_shift'],
              _block_diag2(lp['rwkv_w2']).astype(BF16), _block_diag2(lp['rwkv_a2']).astype(BF16),
              lp['rwkv_g2'].astype(BF16),
              lp['rwkv_w0'].reshape(1, 2 * hd), lp['rwkv_a0'].reshape(1, 2 * hd),
              lp['rwkv_kk'].reshape(1, hd), lp['rwkv_ka'].reshape(1, hd), lp['rwkv_rk'].reshape(1, hd),
              _head_seg_matrix(min(hd, MXU_WIDTH), RWKV_HEAD_DIM)]
    one = jax.ShapeDtypeStruct((m, hd), BF16)
    two = jax.ShapeDtypeStruct((m, 2 * hd), BF16)
    ospec = lambda n: pl.BlockSpec((tile, n), lambda i: (i, 0))
    return pl.pallas_call(
        functools.partial(_rwkv_prep_kernel, tile=tile, n_lat_rows=n_lat_rows, seq=seq, ctx_len=ctx_len),
        grid=(m // tile,),
        in_specs=_halo_specs(tile, width, m) + [full(a) for a in consts],
        out_specs=[ospec(hd)] * 5 + [ospec(2 * hd)] * 3,
        out_shape=[one] * 5 + [jax.ShapeDtypeStruct((m, 2 * hd), F32), two, two],
        compiler_params=_cparams("parallel"),
        name="rwkv_prep",
    )(zr, zr, zr, *consts)


def _split3(x):
    h1 = x.astype(BF16)
    r1 = x - h1.astype(F32)
    h2 = r1.astype(BF16)
    h3 = (r1 - h2.astype(F32)).astype(BF16)
    return h1, h2, h3


def _scan_kernel(*refs):
    ins = (refs[0:6], refs[6:12])
    y_refs = refs[12:14]
    s_ref = refs[14]
    c_len = SCAN_CHUNK
    n_sub = refs[0].shape[0] // c_len
    n_pairs = RWKV_DIM // LANES

    @pl.when(pl.program_id(1) == 0)
    def _():
        s_ref[...] = jnp.zeros_like(s_ref)

    row = lax.broadcasted_iota(jnp.int32, (c_len, c_len), 0)
    col = lax.broadcasted_iota(jnp.int32, (c_len, c_len), 1)
    row_c = lax.broadcasted_iota(jnp.int32, (c_len, LANES), 0)
    col_c = lax.broadcasted_iota(jnp.int32, (c_len, LANES), 1) & (c_len - 1)
    row_4 = lax.broadcasted_iota(jnp.int32, (c_len, 2 * LANES), 0)
    col_4 = lax.broadcasted_iota(jnp.int32, (c_len, 2 * LANES), 1) & (c_len - 1)
    eye_c = jnp.where(row_c == col_c, 1.0, 0.0)
    head0 = lax.broadcasted_iota(jnp.int32, (c_len, LANES), 1) < RWKV_HEAD_DIM
    srow = lax.broadcasted_iota(jnp.int32, (LANES, LANES), 0) // RWKV_HEAD_DIM
    scol = lax.broadcasted_iota(jnp.int32, (LANES, LANES), 1) // RWKV_HEAD_DIM
    same_head = srow == scol

    def by_head(x):
        zero = jnp.zeros_like(x)
        return jnp.concatenate([jnp.where(head0, x, zero), jnp.where(head0, zero, x)], axis=0)

    masks = []
    for d in range(2):
        ahead = (row - col) if d == 0 else (col - row)
        ahead_c = (row_c - col_c) if d == 0 else (col_c - row_c)
        ahead_4 = (row_4 - col_4) if d == 0 else (col_4 - row_4)
        tri = jnp.where(ahead >= 0, 1.0, 0.0).astype(BF16)
        pair_masks = []
        for sh in range(int(math.log2(c_len))):
            same_pair = (row_c >> (sh + 1)) == (col_c >> (sh + 1))
            pair_masks.append(same_pair & ((row_c >> sh) != (col_c >> sh)) & (ahead_c > 0))
        masks.append(dict(pair=pair_masks, strict=ahead_c > 0, incl4=ahead_4 >= 0, tri=tri))

    def scaled_operands(d, g):
        r_ref, lw_ref, k_ref, v_ref, kk_ref, b_ref = ins[d]
        rows = pl.ds(g * c_len, c_len)
        tri = masks[d]['tri']
        lw = lw_ref[rows, :]
        h1, h2, h3 = _split3(lw)
        c = _dot(tri, h1) + _dot(tri, h2) + _dot(tri, h3)
        tot = jnp.sum(lw, axis=0, keepdims=True)
        e_neg = jnp.exp(-c)
        dec = jnp.exp(tot - c)
        f32 = lambda ref: ref[rows, :].astype(F32)
        b_f, k_f = f32(b_ref), f32(k_ref)
        return dict(
            a_t=-f32(kk_ref) * jnp.exp(c - lw), r_t=f32(r_ref) * jnp.exp(c),
            b_t=(b_f * e_neg).astype(BF16), k_t=(k_f * e_neg).astype(BF16),
            b_p=(b_f * dec).astype(BF16), k_p=(k_f * dec).astype(BF16),
            p_tot=jnp.exp(tot), v=v_ref[rows, :])

    lanes_of = lambda p: slice(p * LANES, (p + 1) * LANES)
    sub_of = lambda d, step: step if d == 0 else n_sub - 1 - step
    state = {(d, p): s_ref[d, p] for d in range(2) for p in range(n_pairs)}
    wave_len = min(SCAN_WAVE, n_sub)

    n_waves = n_sub // wave_len
    wave_subs = lambda wave: [(d, sub_of(d, s)) for s in range(wave * wave_len, (wave + 1) * wave_len)
                              for d in range(2)]
    next_ops = {dg: scaled_operands(*dg) for dg in wave_subs(0)}
    for wave in range(n_waves):
        steps = range(wave * wave_len, (wave + 1) * wave_len)
        ops, next_ops = next_ops, {}
        pending = wave_subs(wave + 1) if wave + 1 < n_waves else []
        groups = [(d, g, p) for (d, g) in ops for p in range(n_pairs)]

        scores, v_bd = {}, {}
        for gr in groups:
            d, g, p = gr
            sl = lanes_of(p)
            lhs = jnp.concatenate([ops[d, g]['a_t'][:, sl], ops[d, g]['r_t'][:, sl]], axis=0).astype(BF16)
            rhs = jnp.concatenate([by_head(ops[d, g]['b_t'][:, sl]), by_head(ops[d, g]['k_t'][:, sl])], axis=0)
            scores[gr] = _dot_nt(lhs, rhs)
            v_bd[gr] = by_head(ops[d, g]['v'][:, sl])
        w0, l_mat, x_inv = {}, {}, {}
        for gr in groups:
            d = gr[0]
            top = scores[gr][:c_len]
            l_mat[gr] = top[:, :LANES]
            w0[gr] = _dot(jnp.where(masks[d]['strict'], top[:, LANES:], 0.0).astype(BF16), v_bd[gr])
            x_inv[gr] = eye_c + jnp.where(masks[d]['pair'][0], l_mat[gr], 0.0)

        for level in range(1, int(math.log2(c_len))):
            t1 = {}
            for gr in groups:
                pm = masks[gr[0]]['pair'][level]
                t1[gr] = _dot(jnp.where(pm, l_mat[gr], 0.0).astype(BF16), by_head(x_inv[gr].astype(BF16)))
            for gr in groups:
                x_inv[gr] = x_inv[gr] + _dot(x_inv[gr].astype(BF16), by_head(t1[gr].astype(BF16)))
            if pending:
                dg = pending.pop(0)
                next_ops[dg] = scaled_operands(*dg)
        for dg in pending:
            next_ops[dg] = scaled_operands(*dg)
        xa_u0 = {}
        for gr in groups:
            d, g, p = gr
            rhs = jnp.concatenate([by_head(ops[d, g]['a_t'][:, lanes_of(p)].astype(BF16)),
                                   by_head(w0[gr].astype(BF16))], axis=1)
            xa_u0[gr] = _dot(x_inv[gr].astype(BF16), rhs)
        p_row_of = {dg: i for i, dg in enumerate(ops)}
        p_all = jnp.concatenate([ops[dg]['p_tot'] for dg in ops]
                                + [jnp.zeros((LANES - len(ops), RWKV_DIM), F32)], axis=0)
        p_t = [jnp.transpose(p_all[:, lanes_of(p)]) for p in range(n_pairs)]
        bk_t = {}
        for d, g, p in groups:
            sl = lanes_of(p)
            bk_t[d, g, p] = jnp.transpose(
                jnp.concatenate([ops[d, g]['b_p'][:, sl], ops[d, g]['k_p'][:, sl]], axis=0))
        lhs_t, p_col = {}, {}
        for gr in groups:
            d, g, p = gr
            lhs_t[gr] = jnp.concatenate([xa_u0[gr][:, :LANES].astype(BF16),
                                         ops[d, g]['r_t'][:, lanes_of(p)].astype(BF16)], axis=0)
            r_idx = p_row_of[d, g]
            p_col[gr] = jnp.broadcast_to(p_t[p][:, r_idx:r_idx + 1], (LANES, LANES))

        for step in steps:
            live = [(d, sub_of(d, step), p) for d in range(2) for p in range(n_pairs)]
            prod = {gr: _dot(lhs_t[gr], state[gr[0], gr[2]].astype(BF16)) for gr in live}
            u = {gr: (prod[gr][:c_len] + xa_u0[gr][:, LANES:]).astype(BF16) for gr in live}
            upd = {gr: _dot(bk_t[gr],
                            jnp.concatenate([u[gr], ops[gr[0], gr[1]]['v'][:, lanes_of(gr[2])]], axis=0))
                   for gr in live}
            for gr in live:
                d, g, p = gr
                state[d, p] = p_col[gr] * state[d, p] + jnp.where(same_head, upd[gr], 0.0)
            for gr in live:
                d, g, p = gr
                y = _dot(jnp.where(masks[d]['incl4'], scores[gr][c_len:], 0.0).astype(BF16),
                         jnp.concatenate([by_head(u[gr]), v_bd[gr]], axis=0))
                y_refs[d][pl.ds(g * c_len, c_len), lanes_of(p)] = prod[gr][c_len:] + y
    for (d, p), s_new in state.items():
        s_ref[d, p] = s_new


def _rwkv_scan(r, lw, kd, v, kk, bvec, batch, seq, ctx_len):
    m, hd = r.shape
    n_sub = _pick_tile(SCAN_SUBCHUNKS, ctx_len // SCAN_CHUNK, seq // SCAN_CHUNK)
    c_len = n_sub * SCAN_CHUNK
    n_ctx, n_lat = ctx_len // c_len, seq // c_len
    ctx0 = batch * seq // c_len

    def rowblk(d):
        def fn(b, j):
            jc = j if d == 0 else n_ctx - 1 - j
            jl = (j - n_ctx) if d == 0 else n_lat - 1 - (j - n_ctx)
            return jnp.where(j < n_ctx, ctx0 + b * n_ctx + jc, b * n_lat + jl)
        return fn

    in_specs, args = [], []
    for d in range(2):
        shared = pl.BlockSpec((c_len, hd), lambda b, j, f=rowblk(d): (f(b, j), 0))
        per_dir = pl.BlockSpec((c_len, hd), lambda b, j, f=rowblk(d), d=d: (f(b, j), d))
        in_specs += [shared, per_dir, per_dir, shared, shared, per_dir]
        args += [r, lw, kd, v, kk, bvec]
    out = jax.ShapeDtypeStruct((m, hd), F32)
    return pl.pallas_call(
        _scan_kernel,
        grid=(batch, n_ctx + n_lat),
        in_specs=in_specs,
        out_specs=[pl.BlockSpec((c_len, hd), lambda b, j, f=rowblk(d): (f(b, j), 0)) for d in range(2)],
        out_shape=[out, out],
        scratch_shapes=[pltpu.VMEM((2, hd // LANES, LANES, LANES), F32)],
        compiler_params=_cparams("parallel", "arbitrary"),
        name="rwkv_scan",
    )(*args)


def _merge_kernel(*refs, n_x, n_ya, n_lat_tiles):
    x_refs, ya_refs = refs[:n_x], refs[n_x:n_x + n_ya]
    (yc_ref, yf_ref, yb_ref, bonus_ref, rg_ref, gg_ref, gb_ref, seg_ref,
     gt_ref, pc_ref, pa_ref, pr_ref, wo_ref, gate_ref, o_ref) = refs[n_x + n_ya:]
    d = o_ref.shape[1]
    seg = seg_ref[...]
    ys = yf_ref[...] + yb_ref[...]
    inv_n = 1.0 / RWKV_HEAD_DIM
    mu = _seg_sum(ys, seg) * inv_n
    dlt = ys - mu
    var = _seg_sum(dlt * dlt, seg) * inv_n
    yn = dlt * lax.rsqrt(var + GN_EPS) * gg_ref[...] + gb_ref[...]
    y_rw = ((yn + bonus_ref[...].astype(F32)) * rg_ref[...].astype(F32)).astype(BF16)
    g = jax.nn.sigmoid(gt_ref[...].astype(F32))
    mix = (g[:, 0:d] * _dot(yc_ref[...], pc_ref[...])
           + g[:, d:2 * d] * _dot(_tile_rows(ya_refs, n_lat_tiles), pa_ref[...])
           + g[:, 2 * d:] * _dot(y_rw, pr_ref[...]))
    o_ref[...] = _tile_rows(x_refs, n_lat_tiles) + gate_ref[...] * _dot(mix.astype(BF16), wo_ref[...])


def _merge(x_parts, rows, yc, ya_parts, yf, yb, bonus, rg, gn_g, gn_b, gates, pc, pa, pr, wo, mod_tab, tm,
           row_of_tile):
    d = x_parts[0].shape[1]
    hd = yf.shape[1]
    seg = _head_seg_matrix(min(hd, MXU_WIDTH), RWKV_HEAD_DIM)
    gn_g, gn_b = gn_g.reshape(1, hd), gn_b.reshape(1, hd)
    tok = lambda a: pl.BlockSpec((tm, a.shape[1]), lambda i: (i, 0))
    full = lambda a: pl.BlockSpec(a.shape, lambda i: (0, 0))
    x_specs, x_args = _row_specs(x_parts, tm)
    ya_specs, ya_args = _row_specs(ya_parts, tm)
    return pl.pallas_call(
        functools.partial(_merge_kernel, n_x=len(x_args), n_ya=len(ya_args),
                          n_lat_tiles=ya_parts[0].shape[0] // tm),
        grid=(rows // tm,),
        in_specs=x_specs + ya_specs + [
            tok(yc), tok(yf), tok(yb), tok(bonus), tok(rg), full(gn_g), full(gn_b),
            full(seg), tok(gates), full(pc), full(pa), full(pr), full(wo),
            pl.BlockSpec((None, None, 1, d), lambda i: (row_of_tile(i), 2, 0, 0))],
        out_specs=pl.BlockSpec((tm, d), lambda i: (i, 0)),
        out_shape=jax.ShapeDtypeStruct((rows, d), F32),
        compiler_params=_cparams("parallel"),
        name="merge_out_proj",
    )(*x_args, *ya_args, yc, yf, yb, bonus, rg, gn_g, gn_b, seg, gates, pc, pa, pr, wo, mod_tab)


def _mlp_kernel(*refs, final):
    if final:
        x_ref, g_ref, sh_ref, sc_ref, gate_ref, w1_ref, w2_ref, fg_ref, o_ref, h_ref, acc_ref = refs
    else:
        x_ref, g_ref, sh_ref, sc_ref, gate_ref, w1_ref, w2_ref, o_ref, h_ref, acc_ref = refs
    f = pl.program_id(1)

    @pl.when(f == 0)
    def _():
        h_ref[...] = _modulated_norm(x_ref[...], g_ref[...], sh_ref[...], sc_ref[...]).astype(BF16)
        acc_ref[...] = jnp.zeros_like(acc_ref)

    t = jnp.maximum(_dot(h_ref[...], w1_ref[...]), 0.0)
    acc_ref[...] += _dot((t * t).astype(BF16), w2_ref[...])

    @pl.when(f == pl.num_programs(1) - 1)
    def _():
        y = x_ref[...] + gate_ref[...] * acc_ref[...]
        if final:
            y = y * lax.rsqrt(jnp.mean(y * y, axis=-1, keepdims=True) + NORM_EPS) * fg_ref[...]
        o_ref[...] = y


def _mlp(x, rows, gain, mod_tab, w1, w2, final_g, tm, tf, row_of_tile):
    d = x.shape[1]
    dff = w1.shape[1]
    final = final_g is not None
    tab = lambda which: pl.BlockSpec((None, None, 1, d), lambda i, f: (row_of_tile(i), which, 0, 0))
    in_specs = [pl.BlockSpec((tm, d), lambda i, f: (i, 0)),
                pl.BlockSpec((1, d), lambda i, f: (0, 0)),
                tab(3), tab(4), tab(5),
                pl.BlockSpec((d, tf), lambda i, f: (0, f)),
                pl.BlockSpec((tf, d), lambda i, f: (f, 0))]
    args = [x, gain.reshape(1, d), mod_tab, mod_tab, mod_tab, w1, w2]
    if final:
        in_specs.append(pl.BlockSpec((1, d), lambda i, f: (0, 0)))
        args.append(final_g.reshape(1, d))
    return pl.pallas_call(
        functools.partial(_mlp_kernel, final=final),
        grid=(rows // tm, dff // tf),
        in_specs=in_specs,
        out_specs=pl.BlockSpec((tm, d), lambda i, f: (i, 0)),
        out_shape=jax.ShapeDtypeStruct((rows, d), F32),
        scratch_shapes=[pltpu.VMEM((tm, d), BF16), pltpu.VMEM((tm, d), F32)],
        compiler_params=_cparams("parallel", "arbitrary"),
        name="mlp_final" if final else "mlp",
    )(*args)


def kernel(x, c, ctx, c_ctx, mod_w, mod_b, norm1_g, norm2_g, w_in, conv_dw_w, conv_dw_b, conv_ln_g, conv_ln_b, p_conv, att_lq1, att_lk1, att_lq2, att_lk2, att_subln_g, p_att, rwkv_shift, rwkv_w0, rwkv_w2, rwkv_a0, rwkv_a2, rwkv_g2, rwkv_kk, rwkv_ka, rwkv_rk, rwkv_gn_g, rwkv_gn_b, p_rwkv, w_out, mlp_w1, mlp_w2, final_g):
    batch, seq, d = x.shape
    ctx_len = ctx.shape[1]
    depth = mod_w.shape[0]
    n_lat = batch * seq
    conv_w = 2 * conv_dw_w.shape[2]
    att_w = 3 * ATT_HEADS * ATT_VAL_DIM
    rw_w = 3 * RWKV_DIM + 2 * DECAY_LORA + 2 * ICLR_LORA + GATE_LORA
    assert w_in.shape[2] == conv_w + att_w + rw_w + 3 * d
    assert seq % SCAN_CHUNK == 0 and ctx_len % SCAN_CHUNK == 0 and (batch * seq) % ctx_len == 0

    tm = _pick_tile(1024, seq, batch * ctx_len)
    tmm = _pick_tile(512, seq, batch * ctx_len)
    tc = _pick_tile(256, seq, ctx_len)
    tiles_per_seq = seq // tm
    row_of_tile = lambda i: jnp.minimum(i // tiles_per_seq, batch)
    row_of_tile_m = lambda i: jnp.minimum(i // (seq // tmm), batch)

    xs = (x.reshape(n_lat, d), ctx.reshape(batch * ctx_len, d))
    n_rows = n_lat + batch * ctx_len
    pad = (-(batch + 1)) % 8
    c_all = jnp.concatenate([c, c_ctx[None, :], jnp.zeros((pad, d), F32)], axis=0)
    rope_tabs = _rope_tables(seq)

    for l in range(depth):
        last = l == depth - 1
        lambda_init = 0.8 - 0.6 * math.exp(-0.3 * l)
        mod = _modulation(c_all, mod_w[l], mod_b[l])
        mod_tab = mod[:batch + 1].reshape(batch + 1, 6, 1, d)

        wl = w_in[l].astype(BF16)
        o1, o2, o3 = conv_w, conv_w + att_w, conv_w + att_w + rw_w
        zc, h1 = _norm_project(xs, norm1_g[l], mod_tab, 0, wl[:, :o1], tm, row_of_tile, "norm_proj_conv")
        qkv = _project(h1, wl[:, o1:o2], tm, "proj_qkv")
        zr = _project(h1, wl[:, o2:o3], tm, "proj_rwkv")
        gates = _project(h1, wl[:, o3:], tm, "proj_gates")

        y_conv = _conv_branch(zc, conv_dw_w[l], conv_dw_b[l], conv_ln_g[l], conv_ln_b[l],
                              tc, n_lat, seq, ctx_len)

        lam_vecs = jnp.stack([att_lq1[l], att_lk1[l], att_lq2[l], att_lk2[l]])
        y_att = (_attention(qkv, rope_tabs, lam_vecs, att_subln_g[l], lambda_init, batch, seq, ctx_len, True),
                 None if last else
                 _attention(qkv, None, lam_vecs, att_subln_g[l], lambda_init, batch, seq, ctx_len, False))

        lp = dict(rwkv_shift=rwkv_shift[l], rwkv_w0=rwkv_w0[l], rwkv_w2=rwkv_w2[l], rwkv_a0=rwkv_a0[l],
                  rwkv_a2=rwkv_a2[l], rwkv_g2=rwkv_g2[l], rwkv_kk=rwkv_kk[l], rwkv_ka=rwkv_ka[l],
                  rwkv_rk=rwkv_rk[l])
        r, v, kk, g, bonus, lw, kd, bvec = _rwkv_prep(zr, lp, tc, n_lat, seq, ctx_len)
        yf, yb = _rwkv_scan(r, lw, kd, v, kk, bvec, batch, seq, ctx_len)

        rows = n_lat if last else n_rows
        xm = _merge(xs, rows, y_conv, y_att, yf, yb, bonus, g, rwkv_gn_g[l], rwkv_gn_b[l], gates,
                    p_conv[l].astype(BF16), p_att[l].astype(BF16),
                    p_rwkv[l].astype(BF16), w_out[l].astype(BF16), mod_tab, tmm, row_of_tile_m)
        xs = (_mlp(xm, rows, norm2_g[l], mod_tab, mlp_w1[l].astype(BF16), mlp_w2[l].astype(BF16),
                   final_g if last else None, tm, _pick_tile(MLP_FF_TILE, mlp_w1.shape[2]), row_of_tile), None)
    return xs[0].reshape(batch, seq, d)
```

```python
import functools
import math

import jax
import jax.numpy as jnp
from jax import lax
from jax.experimental import pallas as pl
from jax.experimental.pallas import tpu as pltpu

F32 = jnp.float32
BF16 = jnp.bfloat16

GRID_W = 64
NORM_EPS = 1e-6
CONV_WIDTH = 31
LN_EPS = 1e-5
ATT_HEADS = 4
ATT_HEAD_DIM = 64
ATT_VAL_DIM = 2 * ATT_HEAD_DIM
ROPE_BASE = 10000.0
SUBLN_EPS = 1e-5
RWKV_HEADS = 8
RWKV_HEAD_DIM = 64
RWKV_DIM = RWKV_HEADS * RWKV_HEAD_DIM
DECAY_LORA = 64
ICLR_LORA = 64
GATE_LORA = 128
GN_EPS = 64e-5

LANES = 128
SUBLANES = 8
MXU_WIDTH = 256
HALO = 16
SCAN_CHUNK = 64
SCAN_SUBCHUNKS = 4
SCAN_WAVE = 2
MLP_FF_TILE = 1024
PROJ_ROW_PARTS = 2
ATT_Q_TILE = 2048
ATT_ROW_PARTS = 16
VMEM_LIMIT = 56 << 20


def _cparams(*sem):
    return pltpu.CompilerParams(dimension_semantics=sem, vmem_limit_bytes=VMEM_LIMIT)


def _pick_tile(limit, *sizes):
    t = limit
    while any(s % t for s in sizes):
        t //= 2
    return t


def _dot(a, b):
    return jnp.dot(a, b, preferred_element_type=F32)


def _dot_nt(a, b):
    return lax.dot_general(a, b, (((1,), (1,)), ((), ())), preferred_element_type=F32)


def _dot_tn(a, b):
    return lax.dot_general(a, b, (((0,), (0,)), ((), ())), preferred_element_type=F32)


def _seg_sum(x, seg):
    w = seg.shape[0]
    hi = x.astype(BF16)
    lo = (x - hi.astype(F32)).astype(BF16)
    return jnp.concatenate([_dot(hi[:, j:j + w], seg) + _dot(lo[:, j:j + w], seg)
                            for j in range(0, x.shape[1], w)], axis=1)


def _modulated_norm(x, gain, shift, scale):
    y = x * lax.rsqrt(jnp.mean(x * x, axis=-1, keepdims=True) + NORM_EPS)
    return (y * gain) * (1.0 + scale) + shift


def _mod_kernel(c_ref, w_ref, b_ref, o_ref):
    c = c_ref[...]
    s = c * jax.nn.sigmoid(c)
    o_ref[...] = _dot(s.astype(BF16), w_ref[...].astype(BF16)) + b_ref[...]


def _modulation(c_all, mod_w, mod_b):
    rows, d = c_all.shape
    n = mod_w.shape[1]
    tn = _pick_tile(1024, n // 6)
    return pl.pallas_call(
        _mod_kernel,
        grid=(n // tn,),
        in_specs=[pl.BlockSpec((rows, d), lambda j: (0, 0)),
                  pl.BlockSpec((d, tn), lambda j: (0, j)),
                  pl.BlockSpec((1, tn), lambda j: (0, j))],
        out_specs=pl.BlockSpec((rows, tn), lambda j: (0, j)),
        out_shape=jax.ShapeDtypeStruct((rows, n), F32),
        compiler_params=_cparams("arbitrary"),
        name="adaln_modulation",
    )(c_all, mod_w, mod_b.reshape(1, n))


def _row_specs(parts, tile):
    lat, ctx_rows = parts
    cols = lat.shape[1]
    if ctx_rows is None:
        return [pl.BlockSpec((tile, cols), lambda i: (i, 0))], [lat]
    n_lat_tiles = lat.shape[0] // tile
    return ([pl.BlockSpec((tile, cols), lambda i: (jnp.minimum(i, n_lat_tiles - 1), 0)),
             pl.BlockSpec((tile, cols), lambda i: (jnp.maximum(i - n_lat_tiles, 0), 0))], [lat, ctx_rows])


def _tile_rows(refs, n_lat_tiles, rows=slice(None)):
    if len(refs) == 1:
        return refs[0][rows, :]
    return jnp.where(pl.program_id(0) < n_lat_tiles, refs[0][rows, :], refs[1][rows, :])


def _norm_proj_kernel(*refs, n_x, n_lat_tiles):
    x_refs = refs[:n_x]
    g_ref, sh_ref, sc_ref, w_ref, o_ref, h_ref = refs[n_x:]
    part = o_ref.shape[0] // PROJ_ROW_PARTS
    for i in range(PROJ_ROW_PARTS):
        rows = pl.ds(i * part, part)
        x = _tile_rows(x_refs, n_lat_tiles, rows)
        h = _modulated_norm(x, g_ref[...], sh_ref[...], sc_ref[...]).astype(BF16)
        h_ref[rows, :] = h
        o_ref[rows, :] = _dot(h, w_ref[...]).astype(o_ref.dtype)


def _norm_project(x_parts, gain, mod_tab, which_shift, w, tm, row_of_tile, name):
    m = sum(a.shape[0] for a in x_parts if a is not None)
    d = x_parts[0].shape[1]
    n = w.shape[1]
    x_specs, x_args = _row_specs(x_parts, tm)
    tab = lambda which: pl.BlockSpec((None, None, 1, d), lambda i: (row_of_tile(i), which, 0, 0))
    return pl.pallas_call(
        functools.partial(_norm_proj_kernel, n_x=len(x_args), n_lat_tiles=x_parts[0].shape[0] // tm),
        grid=(m // tm,),
        in_specs=x_specs + [pl.BlockSpec((1, d), lambda i: (0, 0)),
                            tab(which_shift), tab(which_shift + 1),
                            pl.BlockSpec((d, n), lambda i: (0, 0))],
        out_specs=[pl.BlockSpec((tm, n), lambda i: (i, 0)), pl.BlockSpec((tm, d), lambda i: (i, 0))],
        out_shape=[jax.ShapeDtypeStruct((m, n), BF16), jax.ShapeDtypeStruct((m, d), BF16)],
        compiler_params=_cparams("parallel"),
        name=name,
    )(*x_args, gain.reshape(1, d), mod_tab, mod_tab, w)


def _proj_kernel(h_ref, w_ref, o_ref):
    o_ref[...] = _dot(h_ref[...], w_ref[...]).astype(o_ref.dtype)


def _project(h, w, tm, name):
    m, d = h.shape
    n = w.shape[1]
    return pl.pallas_call(
        _proj_kernel,
        grid=(m // tm,),
        in_specs=[pl.BlockSpec((tm, d), lambda i: (i, 0)), pl.BlockSpec((d, n), lambda i: (0, 0))],
        out_specs=pl.BlockSpec((tm, n), lambda i: (i, 0)),
        out_shape=jax.ShapeDtypeStruct((m, n), BF16),
        compiler_params=_cparams("parallel"),
        name=name,
    )(h, w)


def _seq_edges(i, tile, n_lat_rows, seq, ctx_len):
    n_lat_tiles = n_lat_rows // tile
    is_lat = i < n_lat_tiles
    per = jnp.where(is_lat, seq // tile, ctx_len // tile)
    pos = jnp.where(is_lat, i, i - n_lat_tiles) % per
    return pos == 0, pos == per - 1


def _conv_kernel(z_ref, zp_ref, zn_ref, w_ref, b_ref, lg_ref, lb_ref, o_ref, ext_ref, rot_ref,
                 *, tile, n_lat_rows, seq, ctx_len):
    first, last = _seq_edges(pl.program_id(0), tile, n_lat_rows, seq, ctx_len)
    cd = o_ref.shape[1]

    def glu(z):
        z = z.astype(F32)
        return z[:, :cd] * jax.nn.sigmoid(z[:, cd:])

    ext_ref[0:HALO, :] = jnp.where(first, 0.0, glu(zp_ref[...]))
    ext_ref[HALO:HALO + tile, :] = glu(z_ref[...])
    ext_ref[HALO + tile:, :] = jnp.where(last, 0.0, glu(zn_ref[...]))
    span = rot_ref.shape[1]
    for r in range(SUBLANES):
        rot_ref[r] = ext_ref[pl.ds(r, span), :]
    half = CONV_WIDTH // 2
    acc = jnp.zeros((tile, cd), F32)
    for k in range(CONV_WIDTH):
        start = HALO - half + k
        acc = acc + w_ref[k:k + 1, :] * rot_ref[start % SUBLANES, pl.ds(start - start % SUBLANES, tile), :]
    u = acc + b_ref[...]
    mu = jnp.mean(u, axis=-1, keepdims=True)
    dlt = u - mu
    var = jnp.mean(dlt * dlt, axis=-1, keepdims=True)
    y = dlt * lax.rsqrt(var + LN_EPS) * lg_ref[...] + lb_ref[...]
    o_ref[...] = (y * jax.nn.sigmoid(y)).astype(o_ref.dtype)


def _halo_specs(tile, width, m):
    per = tile // HALO
    last_blk = m // HALO - 1
    return [pl.BlockSpec((tile, width), lambda i: (i, 0)),
            pl.BlockSpec((HALO, width), lambda i: (jnp.maximum(i * per - 1, 0), 0)),
            pl.BlockSpec((HALO, width), lambda i: (jnp.minimum((i + 1) * per, last_blk), 0))]


def _conv_branch(zc, dw_w, dw_b, ln_g, ln_b, tile, n_lat_rows, seq, ctx_len):
    m, width = zc.shape
    cd = width // 2
    vec = pl.BlockSpec((1, cd), lambda i: (0, 0))
    return pl.pallas_call(
        functools.partial(_conv_kernel, tile=tile, n_lat_rows=n_lat_rows, seq=seq, ctx_len=ctx_len),
        grid=(m // tile,),
        in_specs=_halo_specs(tile, width, m) + [pl.BlockSpec((CONV_WIDTH, cd), lambda i: (0, 0)),
                                                vec, vec, vec],
        out_specs=pl.BlockSpec((tile, cd), lambda i: (i, 0)),
        out_shape=jax.ShapeDtypeStruct((m, cd), BF16),
        scratch_shapes=[pltpu.VMEM((tile + 2 * HALO, cd), F32),
                        pltpu.VMEM((SUBLANES, tile + 2 * HALO - SUBLANES, cd), F32)],
        compiler_params=_cparams("parallel"),
        name="conv_branch",
    )(zc, zc, zc, dw_w, dw_b.reshape(1, cd), ln_g.reshape(1, cd), ln_b.reshape(1, cd))


def _rope(x, cos, sin_next, sin_prev):
    return x * cos + pltpu.roll(x, LANES - 1, 1) * sin_next + pltpu.roll(x, 1, 1) * sin_prev


def _attn_kernel(*refs, lambda_init, n_ctx, with_lat):
    if with_lat:
        (q_ref, kc_ref, vc_ref, kl_ref, vl_ref, cq_ref, snq_ref, spq_ref, ck_ref, snk_ref, spk_ref,
         lam_ref, g_ref, o_ref, ks_ref, vs_ref) = refs
    else:
        q_ref, kc_ref, vc_ref, lam_ref, g_ref, o_ref, ks_ref, vs_ref = refs
    dv = ATT_VAL_DIM

    @pl.when(pl.program_id(2) == 0)
    def _():
        vs_ref[:, dv:] = jnp.ones((vs_ref.shape[0], dv), BF16)
        ks_ref[0:n_ctx, :] = kc_ref[...]
        vs_ref[0:n_ctx, 0:dv] = vc_ref[...]
        if with_lat:
            k = _rope(kl_ref[...].astype(F32), ck_ref[...], snk_ref[...], spk_ref[...])
            ks_ref[n_ctx:, :] = k.astype(BF16)
            vs_ref[n_ctx:, 0:dv] = vl_ref[...]

    tq = q_ref.shape[0]
    q = q_ref[...].astype(F32)
    if with_lat:
        q = _rope(q, cq_ref[...], snq_ref[...], spq_ref[...])
    q = q * (math.log2(math.e) / math.sqrt(ATT_HEAD_DIM))
    first_map = lax.broadcasted_iota(jnp.int32, q.shape, 1) < ATT_HEAD_DIM
    qs = jnp.concatenate([jnp.where(first_map, q, 0.0), jnp.where(first_map, 0.0, q)], axis=0).astype(BF16)
    part = (2 * tq) // ATT_ROW_PARTS
    scores = lambda i: _dot_nt(qs[i * part:(i + 1) * part], ks_ref[...])
    s_next = scores(0)
    outs = []
    for i in range(ATT_ROW_PARTS):
        s = s_next
        if i + 1 < ATT_ROW_PARTS:
            s_next = scores(i + 1)
        p = jnp.exp2((s - jnp.max(s, axis=-1, keepdims=True)).astype(BF16))
        outs.append(_dot(p, vs_ref[...]))
    ol = jnp.concatenate(outs, axis=0)
    o = ol[:, 0:dv] / ol[:, dv:]
    lq = lam_ref[...]
    lam = (jnp.exp(jnp.sum(lq[0:1] * lq[1:2], axis=-1, keepdims=True))
           - jnp.exp(jnp.sum(lq[2:3] * lq[3:4], axis=-1, keepdims=True)) + lambda_init)
    a = o[:tq] - lam * o[tq:]
    y = a * lax.rsqrt(jnp.mean(a * a, axis=-1, keepdims=True) + SUBLN_EPS) * g_ref[...]
    o_ref[...] = (y * (1.0 - lambda_init)).astype(o_ref.dtype)


def _attention(qkv, rope_tabs, lam_vecs, subln_g, lambda_init, batch, seq, ctx_len, latent_queries):
    h_ = ATT_HEADS
    ctx_blk0 = batch * seq // ctx_len
    tq = _pick_tile(ATT_Q_TILE, seq if latent_queries else ctx_len)
    nq = (seq if latent_queries else ctx_len) // tq
    q0 = 0 if latent_queries else batch * seq // tq
    blk = lambda rows, fn: pl.BlockSpec((rows, LANES), fn)
    in_specs = [blk(tq, lambda b, h, i: (q0 + b * nq + i, h)),
                blk(ctx_len, lambda b, h, i: (ctx_blk0 + b, h_ + h)),
                blk(ctx_len, lambda b, h, i: (ctx_blk0 + b, 2 * h_ + h))]
    args = [qkv, qkv, qkv]
    n_keys = ctx_len
    if latent_queries:
        n_keys += seq
        in_specs += [blk(seq, lambda b, h, i: (b, h_ + h)), blk(seq, lambda b, h, i: (b, 2 * h_ + h))]
        in_specs += [blk(tq, lambda b, h, i: (i, 0))] * 3 + [blk(seq, lambda b, h, i: (0, 0))] * 3
        args += [qkv, qkv] + list(rope_tabs) + list(rope_tabs)
    in_specs += [pl.BlockSpec((4, ATT_HEAD_DIM), lambda b, h, i: (0, 0)),
                 pl.BlockSpec((1, ATT_VAL_DIM), lambda b, h, i: (0, 0))]
    args += [lam_vecs, subln_g.reshape(1, ATT_VAL_DIM)]
    out_rows = batch * (seq if latent_queries else ctx_len)
    return pl.pallas_call(
        functools.partial(_attn_kernel, lambda_init=lambda_init, n_ctx=ctx_len, with_lat=latent_queries),
        grid=(batch, h_, nq),
        in_specs=in_specs,
        out_specs=blk(tq, lambda b, h, i: (b * nq + i, h)),
        out_shape=jax.ShapeDtypeStruct((out_rows, h_ * ATT_VAL_DIM), BF16),
        scratch_shapes=[pltpu.VMEM((n_keys, LANES), BF16), pltpu.VMEM((n_keys, 2 * ATT_VAL_DIM), BF16)],
        compiler_params=_cparams("parallel", "parallel", "arbitrary"),
        name="diff_attention_latent" if latent_queries else "diff_attention_context",
    )(*args)


def _rope_tables(seq):
    t = jnp.arange(seq, dtype=jnp.int32)
    row = (t // GRID_W).astype(F32)
    col = (t % GRID_W).astype(F32)
    n_pairs = ATT_HEAD_DIM // 4
    inv_freq = ROPE_BASE ** (-jnp.arange(n_pairs, dtype=F32) / n_pairs)
    ang = jnp.concatenate([row[:, None] * inv_freq, col[:, None] * inv_freq], axis=-1)
    cos = jnp.repeat(jnp.cos(ang), 2, axis=-1)
    sin = jnp.repeat(jnp.sin(ang), 2, axis=-1)
    even = (jnp.arange(ATT_HEAD_DIM) % 2 == 0)[None, :]
    both = lambda a: jnp.concatenate([a, a], axis=-1)
    return both(cos), both(jnp.where(even, -sin, 0.0)), both(jnp.where(even, 0.0, sin))


def _rwkv_prep_kernel(z_ref, zp_ref, zn_ref, smat_ref, sh_ref, w2_ref, a2_ref, g2_ref, w0_ref, a0_ref,
                      kkw_ref, ka_ref, rk_ref, seg_ref,
                      r_o, v_o, kk_o, g_o, bonus_o, lw_o, kd_o, b_o,
                      *, tile, n_lat_rows, seq, ctx_len):
    first, last = _seq_edges(pl.program_id(0), tile, n_lat_rows, seq, ctx_len)
    hd = RWKV_DIM
    zc = z_ref[...]
    shifted = _dot(smat_ref[...], zc)
    prev_row = jnp.where(first, 0.0, zp_ref[...].astype(F32)[HALO - 1:HALO, :])
    next_row = jnp.where(last, 0.0, zn_ref[...].astype(F32)[0:1, :])
    t_idx = lax.broadcasted_iota(jnp.int32, (tile, 1), 0)
    z = (sh_ref[0:1, :] * jnp.where(t_idx == 0, prev_row, shifted[:tile])
         + sh_ref[1:2, :] * zc.astype(F32)
         + sh_ref[2:3, :] * jnp.where(t_idx == tile - 1, next_row, shifted[tile:]))
    seg = seg_ref[...]
    r = z[:, 0:hd]
    k = z[:, hd:2 * hd]
    v = z[:, 2 * hd:3 * hd]
    o = 3 * hd
    wd = z[:, o:o + 2 * DECAY_LORA]
    ad = z[:, o + 2 * DECAY_LORA:o + 2 * DECAY_LORA + 2 * ICLR_LORA]
    gd = z[:, o + 2 * DECAY_LORA + 2 * ICLR_LORA:]
    kkf = k * kkw_ref[...]
    kk = kkf * lax.rsqrt(jnp.maximum(_seg_sum(kkf * kkf, seg), 1e-24))
    g = _dot(jax.nn.sigmoid(gd).astype(BF16), g2_ref[...])
    wl = w0_ref[...] + _dot(jnp.tanh(wd).astype(BF16), w2_ref[...])
    al = a0_ref[...] + _dot(ad.astype(BF16), a2_ref[...])
    softplus = jnp.maximum(-wl, 0.0) + jnp.log(1.0 + jnp.exp(-jnp.abs(wl)))
    lw = -jnp.exp(-softplus - 0.5)
    a = jax.nn.sigmoid(al)
    r_o[...] = r.astype(r_o.dtype)
    v_o[...] = v.astype(v_o.dtype)
    kk_o[...] = kk.astype(kk_o.dtype)
    g_o[...] = g.astype(g_o.dtype)
    lw_o[...] = lw
    kd_sum = jnp.zeros_like(r)
    for d in range(2):
        a_d = a[:, d * hd:(d + 1) * hd]
        kd = k * (1.0 + (a_d - 1.0) * ka_ref[...])
        kd_o[:, d * hd:(d + 1) * hd] = kd.astype(kd_o.dtype)
        b_o[:, d * hd:(d + 1) * hd] = (kk * a_d).astype(b_o.dtype)
        kd_sum = kd_sum + kd
    bonus_o[...] = (_seg_sum(r * kd_sum * rk_ref[...], seg) * v).astype(bonus_o.dtype)


def _block_diag2(w):
    z = jnp.zeros_like(w[0])
    return jnp.concatenate([jnp.concatenate([w[0], z], axis=1), jnp.concatenate([z, w[1]], axis=1)], axis=0)


def _head_seg_matrix(n, head):
    idx = jnp.arange(n) // head
    return (idx[:, None] == idx[None, :]).astype(BF16)


def _rwkv_prep(zr, lp, tile, n_lat_rows, seq, ctx_len):
    m, width = zr.shape
    hd = RWKV_DIM
    full = lambda a: pl.BlockSpec(a.shape, lambda i: (0,) * a.ndim)
    shift_mat = jnp.concatenate([jnp.eye(tile, k=-1, dtype=BF16), jnp.eye(tile, k=1, dtype=BF16)], axis=0)
    consts = [shift_mat, lp['rwkv_shift'],
              _block_diag2(lp['rwkv_w2']).astype(BF16), _block_diag2(lp['rwkv_a2']).astype(BF16),
              lp['rwkv_g2'].astype(BF16),
              lp['rwkv_w0'].reshape(1, 2 * hd), lp['rwkv_a0'].reshape(1, 2 * hd),
              lp['rwkv_kk'].reshape(1, hd), lp['rwkv_ka'].reshape(1, hd), lp['rwkv_rk'].reshape(1, hd),
              _head_seg_matrix(min(hd, MXU_WIDTH), RWKV_HEAD_DIM)]
    one = jax.ShapeDtypeStruct((m, hd), BF16)
    two = jax.ShapeDtypeStruct((m, 2 * hd), BF16)
    ospec = lambda n: pl.BlockSpec((tile, n), lambda i: (i, 0))
    return pl.pallas_call(
        functools.partial(_rwkv_prep_kernel, tile=tile, n_lat_rows=n_lat_rows, seq=seq, ctx_len=ctx_len),
        grid=(m // tile,),
        in_specs=_halo_specs(tile, width, m) + [full(a) for a in consts],
        out_specs=[ospec(hd)] * 5 + [ospec(2 * hd)] * 3,
        out_shape=[one] * 5 + [jax.ShapeDtypeStruct((m, 2 * hd), F32), two, two],
        compiler_params=_cparams("parallel"),
        name="rwkv_prep",
    )(zr, zr, zr, *consts)


def _split3(x):
    h1 = x.astype(BF16)
    r1 = x - h1.astype(F32)
    h2 = r1.astype(BF16)
    h3 = (r1 - h2.astype(F32)).astype(BF16)
    return h1, h2, h3


def _scan_kernel(*refs):
    ins = (refs[0:6], refs[6:12])
    y_refs = refs[12:14]
    s_ref = refs[14]
    c_len = SCAN_CHUNK
    n_sub = refs[0].shape[0] // c_len
    n_pairs = RWKV_DIM // LANES

    @pl.when(pl.program_id(1) == 0)
    def _():
        s_ref[...] = jnp.zeros_like(s_ref)

    row = lax.broadcasted_iota(jnp.int32, (c_len, c_len), 0)
    col = lax.broadcasted_iota(jnp.int32, (c_len, c_len), 1)
    row_c = lax.broadcasted_iota(jnp.int32, (c_len, LANES), 0)
    col_c = lax.broadcasted_iota(jnp.int32, (c_len, LANES), 1) & (c_len - 1)
    row_4 = lax.broadcasted_iota(jnp.int32, (c_len, 2 * LANES), 0)
    col_4 = lax.broadcasted_iota(jnp.int32, (c_len, 2 * LANES), 1) & (c_len - 1)
    eye_c = jnp.where(row_c == col_c, 1.0, 0.0)
    head0 = lax.broadcasted_iota(jnp.int32, (c_len, LANES), 1) < RWKV_HEAD_DIM
    srow = lax.broadcasted_iota(jnp.int32, (LANES, LANES), 0) // RWKV_HEAD_DIM
    scol = lax.broadcasted_iota(jnp.int32, (LANES, LANES), 1) // RWKV_HEAD_DIM
    same_head = srow == scol

    def by_head(x):
        zero = jnp.zeros_like(x)
        return jnp.concatenate([jnp.where(head0, x, zero), jnp.where(head0, zero, x)], axis=0)

    masks = []
    for d in range(2):
        ahead = (row - col) if d == 0 else (col - row)
        ahead_c = (row_c - col_c) if d == 0 else (col_c - row_c)
        ahead_4 = (row_4 - col_4) if d == 0 else (col_4 - row_4)
        tri = jnp.where(ahead >= 0, 1.0, 0.0).astype(BF16)
        pair_masks = []
        for sh in range(int(math.log2(c_len))):
            same_pair = (row_c >> (sh + 1)) == (col_c >> (sh + 1))
            pair_masks.append(same_pair & ((row_c >> sh) != (col_c >> sh)) & (ahead_c > 0))
        masks.append(dict(pair=pair_masks, strict=ahead_c > 0, incl4=ahead_4 >= 0, tri=tri))

    def scaled_operands(d, g):
        r_ref, lw_ref, k_ref, v_ref, kk_ref, b_ref = ins[d]
        rows = pl.ds(g * c_len, c_len)
        tri = masks[d]['tri']
        lw = lw_ref[rows, :]
        h1, h2, h3 = _split3(lw)
        c = _dot(tri, h1) + _dot(tri, h2) + _dot(tri, h3)
        tot = jnp.sum(lw, axis=0, keepdims=True)
        e_neg = jnp.exp(-c)
        dec = jnp.exp(tot - c)
        f32 = lambda ref: ref[rows, :].astype(F32)
        b_f, k_f = f32(b_ref), f32(k_ref)
        return dict(
            a_t=-f32(kk_ref) * jnp.exp(c - lw), r_t=f32(r_ref) * jnp.exp(c),
            b_t=(b_f * e_neg).astype(BF16), k_t=(k_f * e_neg).astype(BF16),
            b_p=(b_f * dec).astype(BF16), k_p=(k_f * dec).astype(BF16),
            p_tot=jnp.exp(tot), v=v_ref[rows, :])

    lanes_of = lambda p: slice(p * LANES, (p + 1) * LANES)
    sub_of = lambda d, step: step if d == 0 else n_sub - 1 - step
    state = {(d, p): s_ref[d, p] for d in range(2) for p in range(n_pairs)}
    wave_len = min(SCAN_WAVE, n_sub)

    n_waves = n_sub // wave_len
    wave_subs = lambda wave: [(d, sub_of(d, s)) for s in range(wave * wave_len, (wave + 1) * wave_len)
                              for d in range(2)]
    next_ops = {dg: scaled_operands(*dg) for dg in wave_subs(0)}
    for wave in range(n_waves):
        steps = range(wave * wave_len, (wave + 1) * wave_len)
        ops, next_ops = next_ops, {}
        pending = wave_subs(wave + 1) if wave + 1 < n_waves else []
        groups = [(d, g, p) for (d, g) in ops for p in range(n_pairs)]

        scores, v_bd = {}, {}
        for gr in groups:
            d, g, p = gr
            sl = lanes_of(p)
            lhs = jnp.concatenate([ops[d, g]['a_t'][:, sl], ops[d, g]['r_t'][:, sl]], axis=0).astype(BF16)
            rhs = jnp.concatenate([by_head(ops[d, g]['b_t'][:, sl]), by_head(ops[d, g]['k_t'][:, sl])], axis=0)
            scores[gr] = _dot_nt(lhs, rhs)
            v_bd[gr] = by_head(ops[d, g]['v'][:, sl])
        w0, l_mat, x_inv = {}, {}, {}
        for gr in groups:
            d = gr[0]
            top = scores[gr][:c_len]
            l_mat[gr] = top[:, :LANES]
            w0[gr] = _dot(jnp.where(masks[d]['strict'], top[:, LANES:], 0.0).astype(BF16), v_bd[gr])
            x_inv[gr] = eye_c + jnp.where(masks[d]['pair'][0], l_mat[gr], 0.0)

        for level in range(1, int(math.log2(c_len))):
            t1 = {}
            for gr in groups:
                pm = masks[gr[0]]['pair'][level]
                t1[gr] = _dot(jnp.where(pm, l_mat[gr], 0.0).astype(BF16), by_head(x_inv[gr].astype(BF16)))
            for gr in groups:
                x_inv[gr] = x_inv[gr] + _dot(x_inv[gr].astype(BF16), by_head(t1[gr].astype(BF16)))
            if pending:
                dg = pending.pop(0)
                next_ops[dg] = scaled_operands(*dg)
        for dg in pending:
            next_ops[dg] = scaled_operands(*dg)
        xa_u0 = {}
        for gr in groups:
            d, g, p = gr
            rhs = jnp.concatenate([by_head(ops[d, g]['a_t'][:, lanes_of(p)].astype(BF16)),
                                   by_head(w0[gr].astype(BF16))], axis=1)
            xa_u0[gr] = _dot(x_inv[gr].astype(BF16), rhs)
        p_row_of = {dg: i for i, dg in enumerate(ops)}
        p_all = jnp.concatenate([ops[dg]['p_tot'] for dg in ops]
                                + [jnp.zeros((LANES - len(ops), RWKV_DIM), F32)], axis=0)
        p_t = [jnp.transpose(p_all[:, lanes_of(p)]) for p in range(n_pairs)]
        bk_t = {}
        for d, g, p in groups:
            sl = lanes_of(p)
            bk_t[d, g, p] = jnp.transpose(
                jnp.concatenate([ops[d, g]['b_p'][:, sl], ops[d, g]['k_p'][:, sl]], axis=0))
        lhs_t, p_col = {}, {}
        for gr in groups:
            d, g, p = gr
            lhs_t[gr] = jnp.concatenate([xa_u0[gr][:, :LANES].astype(BF16),
                                         ops[d, g]['r_t'][:, lanes_of(p)].astype(BF16)], axis=0)
            r_idx = p_row_of[d, g]
            p_col[gr] = jnp.broadcast_to(p_t[p][:, r_idx:r_idx + 1], (LANES, LANES))

        for step in steps:
            live = [(d, sub_of(d, step), p) for d in range(2) for p in range(n_pairs)]
            prod = {gr: _dot(lhs_t[gr], state[gr[0], gr[2]].astype(BF16)) for gr in live}
            u = {gr: (prod[gr][:c_len] + xa_u0[gr][:, LANES:]).astype(BF16) for gr in live}
            upd = {gr: _dot(bk_t[gr],
                            jnp.concatenate([u[gr], ops[gr[0], gr[1]]['v'][:, lanes_of(gr[2])]], axis=0))
                   for gr in live}
            for gr in live:
                d, g, p = gr
                state[d, p] = p_col[gr] * state[d, p] + jnp.where(same_head, upd[gr], 0.0)
            for gr in live:
                d, g, p = gr
                y = _dot(jnp.where(masks[d]['incl4'], scores[gr][c_len:], 0.0).astype(BF16),
                         jnp.concatenate([by_head(u[gr]), v_bd[gr]], axis=0))
                y_refs[d][pl.ds(g * c_len, c_len), lanes_of(p)] = prod[gr][c_len:] + y
    for (d, p), s_new in state.items():
        s_ref[d, p] = s_new


def _rwkv_scan(r, lw, kd, v, kk, bvec, batch, seq, ctx_len):
    m, hd = r.shape
    n_sub = _pick_tile(SCAN_SUBCHUNKS, ctx_len // SCAN_CHUNK, seq // SCAN_CHUNK)
    c_len = n_sub * SCAN_CHUNK
    n_ctx, n_lat = ctx_len // c_len, seq // c_len
    ctx0 = batch * seq // c_len

    def rowblk(d):
        def fn(b, j):
            jc = j if d == 0 else n_ctx - 1 - j
            jl = (j - n_ctx) if d == 0 else n_lat - 1 - (j - n_ctx)
            return jnp.where(j < n_ctx, ctx0 + b * n_ctx + jc, b * n_lat + jl)
        return fn

    in_specs, args = [], []
    for d in range(2):
        shared = pl.BlockSpec((c_len, hd), lambda b, j, f=rowblk(d): (f(b, j), 0))
        per_dir = pl.BlockSpec((c_len, hd), lambda b, j, f=rowblk(d), d=d: (f(b, j), d))
        in_specs += [shared, per_dir, per_dir, shared, shared, per_dir]
        args += [r, lw, kd, v, kk, bvec]
    out = jax.ShapeDtypeStruct((m, hd), F32)
    return pl.pallas_call(
        _scan_kernel,
        grid=(batch, n_ctx + n_lat),
        in_specs=in_specs,
        out_specs=[pl.BlockSpec((c_len, hd), lambda b, j, f=rowblk(d): (f(b, j), 0)) for d in range(2)],
        out_shape=[out, out],
        scratch_shapes=[pltpu.VMEM((2, hd // LANES, LANES, LANES), F32)],
        compiler_params=_cparams("parallel", "arbitrary"),
        name="rwkv_scan",
    )(*args)


def _merge_kernel(*refs, n_x, n_ya, n_lat_tiles):
    x_refs, ya_refs = refs[:n_x], refs[n_x:n_x + n_ya]
    (yc_ref, yf_ref, yb_ref, bonus_ref, rg_ref, gg_ref, gb_ref, seg_ref,
     gt_ref, pc_ref, pa_ref, pr_ref, wo_ref, gate_ref, n2_ref, sh2_ref, sc2_ref, o_ref, h2_ref) = refs[n_x + n_ya:]
    d = o_ref.shape[1]
    seg = seg_ref[...]
    inv_n = 1.0 / RWKV_HEAD_DIM
    part = o_ref.shape[0] // PROJ_ROW_PARTS
    for i in range(PROJ_ROW_PARTS):
        rows = pl.ds(i * part, part)
        ys = yf_ref[rows, :] + yb_ref[rows, :]
        mu = _seg_sum(ys, seg) * inv_n
        dlt = ys - mu
        var = _seg_sum(dlt * dlt, seg) * inv_n
        yn = dlt * lax.rsqrt(var + GN_EPS) * gg_ref[...] + gb_ref[...]
        y_rw = ((yn + bonus_ref[rows, :].astype(F32)) * rg_ref[rows, :].astype(F32)).astype(BF16)
        g = jax.nn.sigmoid(gt_ref[rows, :].astype(F32))
        mix = (g[:, 0:d] * _dot(yc_ref[rows, :], pc_ref[...])
               + g[:, d:2 * d] * _dot(_tile_rows(ya_refs, n_lat_tiles, rows), pa_ref[...])
               + g[:, 2 * d:] * _dot(y_rw, pr_ref[...]))
        x_new = _tile_rows(x_refs, n_lat_tiles, rows) + gate_ref[...] * _dot(mix.astype(BF16), wo_ref[...])
        o_ref[rows, :] = x_new
        h2_ref[rows, :] = _modulated_norm(x_new, n2_ref[...], sh2_ref[...], sc2_ref[...]).astype(BF16)


def _merge(x_parts, rows, yc, ya_parts, yf, yb, bonus, rg, gn_g, gn_b, gates, pc, pa, pr, wo, norm2_g, mod_tab, tm,
           row_of_tile):
    d = x_parts[0].shape[1]
    hd = yf.shape[1]
    seg = _head_seg_matrix(min(hd, MXU_WIDTH), RWKV_HEAD_DIM)
    gn_g, gn_b = gn_g.reshape(1, hd), gn_b.reshape(1, hd)
    tok = lambda a: pl.BlockSpec((tm, a.shape[1]), lambda i: (i, 0))
    full = lambda a: pl.BlockSpec(a.shape, lambda i: (0, 0))
    x_specs, x_args = _row_specs(x_parts, tm)
    ya_specs, ya_args = _row_specs(ya_parts, tm)
    tab = lambda which: pl.BlockSpec((None, None, 1, d), lambda i: (row_of_tile(i), which, 0, 0))
    return pl.pallas_call(
        functools.partial(_merge_kernel, n_x=len(x_args), n_ya=len(ya_args),
                          n_lat_tiles=ya_parts[0].shape[0] // tm),
        grid=(rows // tm,),
        in_specs=x_specs + ya_specs + [
            tok(yc), tok(yf), tok(yb), tok(bonus), tok(rg), full(gn_g), full(gn_b),
            full(seg), tok(gates), full(pc), full(pa), full(pr), full(wo),
            tab(2), pl.BlockSpec((1, d), lambda i: (0, 0)), tab(3), tab(4)],
        out_specs=[pl.BlockSpec((tm, d), lambda i: (i, 0))] * 2,
        out_shape=[jax.ShapeDtypeStruct((rows, d), F32), jax.ShapeDtypeStruct((rows, d), BF16)],
        compiler_params=_cparams("parallel"),
        name="merge_out_proj",
    )(*x_args, *ya_args, yc, yf, yb, bonus, rg, gn_g, gn_b, seg, gates, pc, pa, pr, wo,
      mod_tab, norm2_g.reshape(1, d), mod_tab, mod_tab)


def _mlp_kernel(*refs, final):
    if final:
        x_ref, h_ref, gate_ref, w1_ref, w2_ref, fg_ref, o_ref, acc_ref = refs
    else:
        x_ref, h_ref, gate_ref, w1_ref, w2_ref, o_ref, acc_ref = refs
    f = pl.program_id(1)

    @pl.when(f == 0)
    def _():
        acc_ref[...] = jnp.zeros_like(acc_ref)

    t = jnp.maximum(_dot(h_ref[...], w1_ref[...]), 0.0)
    acc_ref[...] += _dot((t * t).astype(BF16), w2_ref[...])

    @pl.when(f == pl.num_programs(1) - 1)
    def _():
        y = x_ref[...] + gate_ref[...] * acc_ref[...]
        if final:
            y = y * lax.rsqrt(jnp.mean(y * y, axis=-1, keepdims=True) + NORM_EPS) * fg_ref[...]
        o_ref[...] = y


def _mlp(x, h, rows, mod_tab, w1, w2, final_g, tm, tf, row_of_tile):
    d = x.shape[1]
    dff = w1.shape[1]
    final = final_g is not None
    tab = lambda which: pl.BlockSpec((None, None, 1, d), lambda i, f: (row_of_tile(i), which, 0, 0))
    in_specs = [pl.BlockSpec((tm, d), lambda i, f: (i, 0)),
                pl.BlockSpec((tm, d), lambda i, f: (i, 0)),
                tab(5),
                pl.BlockSpec((d, tf), lambda i, f: (0, f)),
                pl.BlockSpec((tf, d), lambda i, f: (f, 0))]
    args = [x, h, mod_tab, w1, w2]
    if final:
        in_specs.append(pl.BlockSpec((1, d), lambda i, f: (0, 0)))
        args.append(final_g.reshape(1, d))
    return pl.pallas_call(
        functools.partial(_mlp_kernel, final=final),
        grid=(rows // tm, dff // tf),
        in_specs=in_specs,
        out_specs=pl.BlockSpec((tm, d), lambda i, f: (i, 0)),
        out_shape=jax.ShapeDtypeStruct((rows, d), F32),
        scratch_shapes=[pltpu.VMEM((tm, d), F32)],
        compiler_params=_cparams("parallel", "arbitrary"),
        name="mlp_final" if final else "mlp",
    )(*args)


def kernel(x, c, ctx, c_ctx, mod_w, mod_b, norm1_g, norm2_g, w_in, conv_dw_w, conv_dw_b, conv_ln_g, conv_ln_b, p_conv, att_lq1, att_lk1, att_lq2, att_lk2, att_subln_g, p_att, rwkv_shift, rwkv_w0, rwkv_w2, rwkv_a0, rwkv_a2, rwkv_g2, rwkv_kk, rwkv_ka, rwkv_rk, rwkv_gn_g, rwkv_gn_b, p_rwkv, w_out, mlp_w1, mlp_w2, final_g):
    batch, seq, d = x.shape
    ctx_len = ctx.shape[1]
    depth = mod_w.shape[0]
    n_lat = batch * seq
    conv_w = 2 * conv_dw_w.shape[2]
    att_w = 3 * ATT_HEADS * ATT_VAL_DIM
    rw_w = 3 * RWKV_DIM + 2 * DECAY_LORA + 2 * ICLR_LORA + GATE_LORA
    assert w_in.shape[2] == conv_w + att_w + rw_w + 3 * d
    assert seq % SCAN_CHUNK == 0 and ctx_len % SCAN_CHUNK == 0 and (batch * seq) % ctx_len == 0

    tm = _pick_tile(1024, seq, batch * ctx_len)
    tmm = _pick_tile(512, seq, batch * ctx_len)
    tc = _pick_tile(256, seq, ctx_len)
    tiles_per_seq = seq // tm
    row_of_tile = lambda i: jnp.minimum(i // tiles_per_seq, batch)
    row_of_tile_m = lambda i: jnp.minimum(i // (seq // tmm), batch)

    xs = (x.reshape(n_lat, d), ctx.reshape(batch * ctx_len, d))
    n_rows = n_lat + batch * ctx_len
    pad = (-(batch + 1)) % 8
    c_all = jnp.concatenate([c, c_ctx[None, :], jnp.zeros((pad, d), F32)], axis=0)
    rope_tabs = _rope_tables(seq)

    for l in range(depth):
        last = l == depth - 1
        lambda_init = 0.8 - 0.6 * math.exp(-0.3 * l)
        mod = _modulation(c_all, mod_w[l], mod_b[l])
        mod_tab = mod[:batch + 1].reshape(batch + 1, 6, 1, d)

        wl = w_in[l].astype(BF16)
        o1, o2, o3 = conv_w, conv_w + att_w, conv_w + att_w + rw_w
        zc, h1 = _norm_project(xs, norm1_g[l], mod_tab, 0, wl[:, :o1], tm, row_of_tile, "norm_proj_conv")
        qkv = _project(h1, wl[:, o1:o2], tm, "proj_qkv")
        zr = _project(h1, wl[:, o2:o3], tm, "proj_rwkv")
        gates = _project(h1, wl[:, o3:], tm, "proj_gates")

        y_conv = _conv_branch(zc, conv_dw_w[l], conv_dw_b[l], conv_ln_g[l], conv_ln_b[l],
                              tc, n_lat, seq, ctx_len)

        lam_vecs = jnp.stack([att_lq1[l], att_lk1[l], att_lq2[l], att_lk2[l]])
        y_att = (_attention(qkv, rope_tabs, lam_vecs, att_subln_g[l], lambda_init, batch, seq, ctx_len, True),
                 None if last else
                 _attention(qkv, None, lam_vecs, att_subln_g[l], lambda_init, batch, seq, ctx_len, False))

        lp = dict(rwkv_shift=rwkv_shift[l], rwkv_w0=rwkv_w0[l], rwkv_w2=rwkv_w2[l], rwkv_a0=rwkv_a0[l],
                  rwkv_a2=rwkv_a2[l], rwkv_g2=rwkv_g2[l], rwkv_kk=rwkv_kk[l], rwkv_ka=rwkv_ka[l],
                  rwkv_rk=rwkv_rk[l])
        r, v, kk, g, bonus, lw, kd, bvec = _rwkv_prep(zr, lp, tc, n_lat, seq, ctx_len)
        yf, yb = _rwkv_scan(r, lw, kd, v, kk, bvec, batch, seq, ctx_len)

        rows = n_lat if last else n_rows
        xm, h2 = _merge(xs, rows, y_conv, y_att, yf, yb, bonus, g, rwkv_gn_g[l], rwkv_gn_b[l], gates,
                        p_conv[l].astype(BF16), p_att[l].astype(BF16),
                        p_rwkv[l].astype(BF16), w_out[l].astype(BF16), norm2_g[l], mod_tab, tmm, row_of_tile_m)
        xs = (_mlp(xm, h2, rows, mod_tab, mlp_w1[l].astype(BF16), mlp_w2[l].astype(BF16),
                   final_g if last else None, tm, _pick_tile(MLP_FF_TILE, mlp_w1.shape[2]), row_of_tile), None)
    return xs[0].reshape(batch, seq, d)
```
